```python
import jax, jax.numpy as jnp
from jax import lax
import numpy as np

D_MODEL = 1024
BATCH = 2
SEQ = 8192
DEPTH = 4

CHUNK = 64
Q_BLOCK = 128

MLA_HEADS = 8
MLA_NOPE = 64
MLA_ROPE = 32
MLA_V = 64
Q_LORA = 384
KV_LORA = 256
ROPE_THETA = 10000.0

FOX_HEADS = 8
FOX_HEAD_DIM = 64

CONV_CHANNELS = 512
CONV_WIDTH = 31

FFN_HIDDEN = -(-8 * D_MODEL // (3 * 256)) * 256

N_BRANCHES = 3
RMS_EPS = 1e-6
LN_EPS = 1e-5

IN_SIZES = (
    Q_LORA,
    KV_LORA,
    MLA_ROPE,
    FOX_HEADS * FOX_HEAD_DIM,
    FOX_HEADS * FOX_HEAD_DIM,
    FOX_HEADS * FOX_HEAD_DIM,
    FOX_HEADS,
    2 * CONV_CHANNELS,
    N_BRANCHES * D_MODEL,
)
IN_WIDTH = int(sum(IN_SIZES))
IN_SPLITS = [int(v) for v in np.cumsum(IN_SIZES)[:-1]]

kernel_name = "hybrid_mla_fox_conformer_gated"


def rmsnorm(x, g):
    xf = x.astype(jnp.float32)
    y = xf * lax.rsqrt(jnp.mean(xf * xf, axis=-1, keepdims=True) + RMS_EPS)
    return (y * g.astype(jnp.float32)).astype(x.dtype)


def layernorm(x, g, b):
    xf = x.astype(jnp.float32)
    mu = jnp.mean(xf, axis=-1, keepdims=True)
    var = jnp.mean(jnp.square(xf - mu), axis=-1, keepdims=True)
    y = (xf - mu) * lax.rsqrt(var + LN_EPS)
    return (y * g.astype(jnp.float32) + b.astype(jnp.float32)).astype(x.dtype)


def apply_rope(x, cos, sin):
    half = x.shape[-1] // 2
    x1 = x[..., :half].astype(jnp.float32)
    x2 = x[..., half:].astype(jnp.float32)
    return jnp.concatenate([x1 * cos - x2 * sin, x2 * cos + x1 * sin], axis=-1).astype(x.dtype)


def block_attention(q, k, v, scale, diag_mask, cum=None):
    S = q.shape[2]
    outs = []
    for q0 in range(0, S, Q_BLOCK):
        q1 = q0 + Q_BLOCK
        s = jnp.einsum('bhqd,bhkd->bhqk', q[:, :, q0:q1], k[:, :, :q1]).astype(jnp.float32) * scale
        if cum is not None:
            s = s + cum[:, :, q0:q1, None] - cum[:, :, None, :q1]
        mask = jnp.concatenate([jnp.ones((Q_BLOCK, q0), dtype=bool), diag_mask], axis=1)
        s = jnp.where(mask, s, -jnp.inf)
        p = jax.nn.softmax(s, axis=-1).astype(v.dtype)
        outs.append(jnp.einsum('bhqk,bhkd->bhqd', p, v[:, :, :q1]))
    return jnp.concatenate(outs, axis=2)


def causal_depthwise_conv(u, w, b):
    y = lax.conv_general_dilated(
        u, w[:, None, :], window_strides=(1,), padding=[(CONV_WIDTH - 1, 0)],
        dimension_numbers=('NWC', 'WIO', 'NWC'), feature_group_count=u.shape[-1])
    return y + b


def setup_inputs(seed: int = 0) -> dict:
    key = jax.random.key(seed)
    ks = jax.random.split(key, 24)
    f32 = jnp.float32
    L = DEPTH

    def nrm(k, shape, fan_in, scale=1.0):
        return jax.random.normal(k, shape, f32) * (scale * fan_in ** -0.5)

    def gain(k, shape):
        return 1.0 + 0.05 * jax.random.normal(k, shape, f32)

    out_scale = (2.0 * DEPTH) ** -0.5
    x = jax.random.normal(ks[0], (BATCH, SEQ, D_MODEL), f32)
    positions = (jax.random.randint(ks[1], (BATCH, 1), 0, 4096, dtype=jnp.int32)
                 + jnp.arange(SEQ, dtype=jnp.int32)[None, :])
    b_forget = (jnp.linspace(1.0, 6.0, FOX_HEADS, dtype=f32)[None, :]
                + 0.1 * jax.random.normal(ks[10], (L, FOX_HEADS), f32))
    return {
        "x": x,
        "positions": positions,
        "norm_mix_g": gain(ks[2], (L, D_MODEL)),
        "w_in": nrm(ks[3], (L, D_MODEL, IN_WIDTH), D_MODEL),
        "b_gate": 0.01 * jax.random.normal(ks[4], (L, N_BRANCHES * D_MODEL), f32),
        "q_norm_g": gain(ks[5], (L, Q_LORA)),
        "w_uq": nrm(ks[6], (L, Q_LORA, MLA_HEADS * (MLA_NOPE + MLA_ROPE)), Q_LORA),
        "kv_norm_g": gain(ks[7], (L, KV_LORA)),
        "w_ukv": nrm(ks[8], (L, KV_LORA, MLA_HEADS * (MLA_NOPE + MLA_V)), KV_LORA),
        "b_forget": b_forget,
        "dw_kernel": nrm(ks[11], (L, CONV_WIDTH, CONV_CHANNELS), CONV_WIDTH),
        "dw_bias": 0.01 * jax.random.normal(ks[12], (L, CONV_CHANNELS), f32),
        "conv_ln_g": gain(ks[13], (L, CONV_CHANNELS)),
        "conv_ln_b": 0.01 * jax.random.normal(ks[14], (L, CONV_CHANNELS), f32),
        "w_bo_a": nrm(ks[15], (L, MLA_HEADS * MLA_V, D_MODEL), MLA_HEADS * MLA_V),
        "w_bo_b": nrm(ks[16], (L, FOX_HEADS * FOX_HEAD_DIM, D_MODEL), FOX_HEADS * FOX_HEAD_DIM),
        "w_bo_c": nrm(ks[17], (L, CONV_CHANNELS, D_MODEL), CONV_CHANNELS),
        "w_out": nrm(ks[18], (L, D_MODEL, D_MODEL), D_MODEL, out_scale),
        "norm_ffn_g": gain(ks[19], (L, D_MODEL)),
        "w_ffn_gate": nrm(ks[20], (L, D_MODEL, FFN_HIDDEN), D_MODEL),
        "w_ffn_up": nrm(ks[21], (L, D_MODEL, FFN_HIDDEN), D_MODEL),
        "w_ffn_down": nrm(ks[22], (L, FFN_HIDDEN, D_MODEL), FFN_HIDDEN, out_scale),
        "final_norm_g": gain(ks[23], (D_MODEL,)),
    }


def reference(x, positions, norm_mix_g, w_in, b_gate, q_norm_g, w_uq, kv_norm_g, w_ukv,
              b_forget, dw_kernel, dw_bias, conv_ln_g, conv_ln_b, w_bo_a, w_bo_b, w_bo_c,
              w_out, norm_ffn_g, w_ffn_gate, w_ffn_up, w_ffn_down, final_norm_g):
    B, S, _ = x.shape

    inv_freq = 1.0 / (ROPE_THETA ** (jnp.arange(0, MLA_ROPE, 2, dtype=jnp.float32) / MLA_ROPE))
    ang = positions.astype(jnp.float32)[..., None] * inv_freq
    cos, sin = jnp.cos(ang), jnp.sin(ang)

    idx = jnp.arange(Q_BLOCK)
    chunk_mask = (idx[:, None] // CHUNK) >= (idx[None, :] // CHUNK)
    frame_mask = idx[:, None] >= idx[None, :]

    mla_scale = (MLA_NOPE + MLA_ROPE) ** -0.5
    fox_scale = FOX_HEAD_DIM ** -0.5

    for l in range(DEPTH):
        h = rmsnorm(x, norm_mix_g[l])
        proj = jnp.einsum('bsd,dn->bsn', h, w_in[l])
        (c_q, c_kv, k_r, q_b, k_b, v_b, f_logit, conv_in, gate_logit) = jnp.split(proj, IN_SPLITS, axis=-1)

        q_a = (rmsnorm(c_q, q_norm_g[l]) @ w_uq[l]).reshape(B, S, MLA_HEADS, MLA_NOPE + MLA_ROPE)
        q_rope = apply_rope(q_a[..., MLA_NOPE:], cos[:, :, None, :], sin[:, :, None, :])
        kv = (rmsnorm(c_kv, kv_norm_g[l]) @ w_ukv[l]).reshape(B, S, MLA_HEADS, MLA_NOPE + MLA_V)
        k_rope = apply_rope(k_r, cos, sin)
        k_rope = jnp.broadcast_to(k_rope[:, :, None, :], (B, S, MLA_HEADS, MLA_ROPE))
        qa = jnp.concatenate([q_a[..., :MLA_NOPE], q_rope], axis=-1).transpose(0, 2, 1, 3)
        ka = jnp.concatenate([kv[..., :MLA_NOPE], k_rope], axis=-1).transpose(0, 2, 1, 3)
        va = kv[..., MLA_NOPE:].transpose(0, 2, 1, 3)
        o_a = block_attention(qa, ka, va, mla_scale, chunk_mask)
        o_a = o_a.transpose(0, 2, 1, 3).reshape(B, S, MLA_HEADS * MLA_V) @ w_bo_a[l]

        log_f = jax.nn.log_sigmoid(f_logit.astype(jnp.float32) + b_forget[l].astype(jnp.float32))
        cum = jnp.cumsum(log_f, axis=1).transpose(0, 2, 1)
        qf = q_b.reshape(B, S, FOX_HEADS, FOX_HEAD_DIM).transpose(0, 2, 1, 3)
        kf = k_b.reshape(B, S, FOX_HEADS, FOX_HEAD_DIM).transpose(0, 2, 1, 3)
        vf = v_b.reshape(B, S, FOX_HEADS, FOX_HEAD_DIM).transpose(0, 2, 1, 3)
        o_b = block_attention(qf, kf, vf, fox_scale, frame_mask, cum)
        o_b = o_b.transpose(0, 2, 1, 3).reshape(B, S, FOX_HEADS * FOX_HEAD_DIM) @ w_bo_b[l]

        u = conv_in[..., :CONV_CHANNELS] * jax.nn.sigmoid(conv_in[..., CONV_CHANNELS:])
        u = causal_depthwise_conv(u, dw_kernel[l], dw_bias[l])
        u = jax.nn.silu(layernorm(u, conv_ln_g[l], conv_ln_b[l]))
        o_c = u @ w_bo_c[l]

        g = jax.nn.sigmoid(gate_logit + b_gate[l])
        g_a, g_b, g_c = jnp.split(g, N_BRANCHES, axis=-1)
        y = g_a * o_a + g_b * o_b + g_c * o_c
        x = x + y @ w_out[l]

        h = rmsnorm(x, norm_ffn_g[l])
        ff = jax.nn.silu(h @ w_ffn_gate[l]) * (h @ w_ffn_up[l])
        x = x + ff @ w_ffn_down[l]

    return rmsnorm(x, final_norm_g)
```

```python
import functools
import math

import jax
import jax.numpy as jnp
from jax import lax
from jax.experimental import pallas as pl
from jax.experimental.pallas import tpu as pltpu

F32 = jnp.float32
BF16 = jnp.bfloat16

MLA_HEADS = 8
MLA_NOPE = 64
MLA_ROPE = 32
MLA_V = 64
Q_LORA = 384
KV_LORA = 256
ROPE_THETA = 10000.0
FOX_HEADS = 8
FOX_HEAD_DIM = 64
CONV_CHANNELS = 512
CONV_WIDTH = 31
CHUNK = 64
N_BRANCHES = 3
RMS_EPS = 1e-6
LN_EPS = 1e-5

HEADS = 8
HEAD_V = 64
KPAD = 128
T = 256
HALO = 32
LOG2E = math.log2(math.e)
NEG_BIG = -1e30
VMEM_LIMIT = 56 * 1024 * 1024


def _rms(x, g):
    ms = jnp.mean(x * x, axis=-1, keepdims=True)
    return x * lax.rsqrt(ms + RMS_EPS) * g


def _split3(v):
    hi = v.astype(BF16)
    r = v - hi.astype(F32)
    mid = r.astype(BF16)
    lo = (r - mid.astype(F32)).astype(BF16)
    return hi, mid, lo


def _dot(a, b):
    return jnp.dot(a, b, preferred_element_type=F32)


def _dot_nt(a, b):
    return lax.dot_general(a, b, (((1,), (1,)), ((), ())), preferred_element_type=F32)


def _dot_tn(a, b):
    return lax.dot_general(a, b, (((0,), (0,)), ((), ())), preferred_element_type=F32)


def _params(sem):
    return pltpu.CompilerParams(dimension_semantics=sem, vmem_limit_bytes=VMEM_LIMIT)


def _const_spec(shape):
    n = len(shape)
    return pl.BlockSpec(shape, lambda *_: (0,) * n)


def _rope_kernel(prow_ref, pcol_ref, invc_ref, invl_ref, sgn_ref, cosT_ref, sinT_ref, cos_ref, sin_ref):
    angT = prow_ref[...].astype(F32) * invc_ref[...]
    cosT_ref[...] = jnp.cos(angT)
    sinT_ref[...] = jnp.sin(angT)
    ang = pcol_ref[...].astype(F32) * invl_ref[...]
    sgn = sgn_ref[...]
    cos_ref[...] = jnp.cos(ang) * jnp.abs(sgn)
    sin_ref[...] = jnp.sin(ang) * sgn


def _rope_tables(positions):
    B, S = positions.shape
    half = MLA_ROPE // 2
    inv_freq = 1.0 / (ROPE_THETA ** (jnp.arange(0, MLA_ROPE, 2, dtype=F32) / MLA_ROPE))
    inv_col = inv_freq.reshape(half, 1)
    zeros = jnp.zeros((MLA_NOPE,), F32)
    tail = jnp.zeros((KPAD - MLA_NOPE - MLA_ROPE,), F32)
    inv_lane = jnp.concatenate([zeros, inv_freq, inv_freq, tail]).reshape(1, KPAD)
    ones = jnp.ones((half,), F32)
    sgn_lane = jnp.concatenate([zeros, -ones, ones, tail]).reshape(1, KPAD)
    TS = min(S, 1024)
    return pl.pallas_call(
        _rope_kernel,
        grid=(B, S // TS),
        in_specs=[
            pl.BlockSpec((None, 1, TS), lambda b, i: (b, 0, i)),
            pl.BlockSpec((None, TS, 1), lambda b, i: (b, i, 0)),
            _const_spec((half, 1)),
            _const_spec((1, KPAD)),
            _const_spec((1, KPAD)),
        ],
        out_specs=[
            pl.BlockSpec((None, half, TS), lambda b, i: (b, 0, i)),
            pl.BlockSpec((None, half, TS), lambda b, i: (b, 0, i)),
            pl.BlockSpec((None, TS, KPAD), lambda b, i: (b, i, 0)),
            pl.BlockSpec((None, TS, KPAD), lambda b, i: (b, i, 0)),
        ],
        out_shape=[
            jax.ShapeDtypeStruct((B, half, S), F32),
            jax.ShapeDtypeStruct((B, half, S), F32),
            jax.ShapeDtypeStruct((B, S, KPAD), F32),
            jax.ShapeDtypeStruct((B, S, KPAD), F32),
        ],
        compiler_params=_params(("parallel", "parallel")),
        name="rope_tables",
    )(positions.reshape(B, 1, S), positions.reshape(B, S, 1), inv_col, inv_lane, sgn_lane)


def _mla_in_kernel(x_ref, gmix_ref, wa_ref, qg_ref, kvg_ref, wqT_ref, wkn_ref, wvT_ref,
                   cosT_ref, sinT_ref, cos_ref, sin_ref, qT_ref, k_ref, vT_ref, *, tm):
    h = _rms(x_ref[...], gmix_ref[...]).astype(BF16)
    pa = _dot(h, wa_ref[...])
    cqn = _rms(pa[:, 0:Q_LORA], qg_ref[...]).astype(BF16)
    ckvn = _rms(pa[:, Q_LORA:Q_LORA + KV_LORA], kvg_ref[...]).astype(BF16)
    o = Q_LORA + KV_LORA
    kra = pa[:, o:o + KPAD]
    krb = pa[:, o + KPAD:o + 2 * KPAD]

    qT = _dot_nt(wqT_ref[...], cqn) * (LOG2E * (MLA_NOPE + MLA_ROPE) ** -0.5)
    qT = qT.reshape(HEADS, KPAD, tm)
    half = MLA_ROPE // 2
    x1 = qT[:, MLA_NOPE:MLA_NOPE + half, :]
    x2 = qT[:, MLA_NOPE + half:MLA_NOPE + 2 * half, :]
    c = cosT_ref[...][None]
    s = sinT_ref[...][None]
    qT = jnp.concatenate(
        [qT[:, :MLA_NOPE, :], x1 * c - x2 * s, x2 * c + x1 * s, qT[:, MLA_NOPE + 2 * half:, :]],
        axis=1).astype(BF16)
    vT = _dot_nt(wvT_ref[...], ckvn).astype(BF16)
    for t in range(tm // T):
        qT_ref[t] = qT[:, :, t * T:(t + 1) * T]
        vT_ref[t] = vT[:, t * T:(t + 1) * T]

    krope = kra * cos_ref[...] + krb * sin_ref[...]
    kn = _dot(ckvn, wkn_ref[...])
    for hd in range(HEADS):
        k_ref[hd] = (kn[:, hd * KPAD:(hd + 1) * KPAD] + krope).astype(BF16)


def _mla_in(x, gmix, wa, qg, kvg, wqT, wkn, wvT, cosT, sinT, cos128, sin128, tm):
    B, S, D = x.shape
    nt = tm // T
    kern = functools.partial(_mla_in_kernel, tm=tm)
    half = MLA_ROPE // 2
    return pl.pallas_call(
        kern,
        grid=(B, S // tm),
        in_specs=[
            pl.BlockSpec((None, tm, D), lambda b, i: (b, i, 0)),
            _const_spec(gmix.shape), _const_spec(wa.shape), _const_spec(qg.shape),
            _const_spec(kvg.shape), _const_spec(wqT.shape), _const_spec(wkn.shape),
            _const_spec(wvT.shape),
            pl.BlockSpec((None, half, tm), lambda b, i: (b, 0, i)),
            pl.BlockSpec((None, half, tm), lambda b, i: (b, 0, i)),
            pl.BlockSpec((None, tm, KPAD), lambda b, i: (b, i, 0)),
            pl.BlockSpec((None, tm, KPAD), lambda b, i: (b, i, 0)),
        ],
        out_specs=[
            pl.BlockSpec((None, nt, HEADS, KPAD, T), lambda b, i: (b, i, 0, 0, 0)),
            pl.BlockSpec((None, HEADS, tm, KPAD), lambda b, i: (b, 0, i, 0)),
            pl.BlockSpec((None, nt, HEADS * HEAD_V, T), lambda b, i: (b, i, 0, 0)),
        ],
        out_shape=[
            jax.ShapeDtypeStruct((B, S // T, HEADS, KPAD, T), BF16),
            jax.ShapeDtypeStruct((B, HEADS, S, KPAD), BF16),
            jax.ShapeDtypeStruct((B, S // T, HEADS * HEAD_V, T), BF16),
        ],
        compiler_params=_params(("parallel", "parallel")),
        name="mla_in",
    )(x, gmix, wa, qg, kvg, wqT, wkn, wvT, cosT, sinT, cos128, sin128)


def _fox_in_kernel(x_ref, gmix_ref, wqT_ref, wkp_ref, wvT_ref, wf_ref, bf_ref, tri_ref, e3_ref,
                   qT_ref, k_ref, vT_ref, carry_ref, *, tm):
    @pl.when(pl.program_id(1) == 0)
    def _():
        carry_ref[...] = jnp.zeros_like(carry_ref)

    h = _rms(x_ref[...], gmix_ref[...]).astype(BF16)
    qT = (_dot_nt(wqT_ref[...], h) * (LOG2E * FOX_HEAD_DIM ** -0.5)).astype(BF16)
    qT = qT.reshape(HEADS, FOX_HEAD_DIM, tm)
    vT = _dot_nt(wvT_ref[...], h).astype(BF16)
    row = lax.broadcasted_iota(jnp.int32, (HEADS, 16, T), 1)
    ones_rows = jnp.where(row < 3, 1.0, 0.0).astype(BF16)
    zero_rows = jnp.zeros((HEADS, KPAD - FOX_HEAD_DIM - 16, T), BF16)
    for t in range(tm // T):
        qT_ref[t, :, 0:FOX_HEAD_DIM, :] = qT[:, :, t * T:(t + 1) * T]
        qT_ref[t, :, FOX_HEAD_DIM:FOX_HEAD_DIM + 16, :] = ones_rows
        qT_ref[t, :, FOX_HEAD_DIM + 16:, :] = zero_rows
        vT_ref[t] = vT[:, t * T:(t + 1) * T]

    f = _dot(h, wf_ref[...]) + bf_ref[...]
    lane = lax.broadcasted_iota(jnp.int32, f.shape, 1)
    logf = jnp.where(lane < FOX_HEADS, jax.nn.log_sigmoid(f), 0.0)
    hi, mid, lo = _split3(logf)
    tri = tri_ref[...]
    cum = _dot(tri, hi) + _dot(tri, mid) + _dot(tri, lo) + carry_ref[...]
    carry_ref[...] = cum[tm - 1:tm, :]
    hi, mid, lo = _split3(cum * (-LOG2E))
    place = _dot(jnp.concatenate([hi, mid, lo], axis=1), e3_ref[...])
    kp = _dot(h, wkp_ref[...]) + place
    for hd in range(HEADS):
        k_ref[hd] = kp[:, hd * KPAD:(hd + 1) * KPAD].astype(BF16)


def _fox_in(x, gmix, wqT, wkp, wvT, wf, bf, tm):
    B, S, D = x.shape
    nt = tm // T
    r = lax.broadcasted_iota(jnp.int32, (tm, tm), 0)
    c = lax.broadcasted_iota(jnp.int32, (tm, tm), 1)
    tri = (c <= r).astype(BF16)
    rr = jnp.arange(3 * KPAD)
    cc = jnp.arange(HEADS * KPAD)
    part, hd = rr // KPAD, rr % KPAD
    e3 = ((cc[None, :] == (hd * KPAD + FOX_HEAD_DIM + part)[:, None]) & (hd < HEADS)[:, None]).astype(BF16)
    kern = functools.partial(_fox_in_kernel, tm=tm)
    return pl.pallas_call(
        kern,
        grid=(B, S // tm),
        in_specs=[
            pl.BlockSpec((None, tm, D), lambda b, i: (b, i, 0)),
            _const_spec(gmix.shape), _const_spec(wqT.shape), _const_spec(wkp.shape),
            _const_spec(wvT.shape), _const_spec(wf.shape), _const_spec(bf.shape),
            _const_spec(tri.shape), _const_spec(e3.shape),
        ],
        out_specs=[
            pl.BlockSpec((None, nt, HEADS, KPAD, T), lambda b, i: (b, i, 0, 0, 0)),
            pl.BlockSpec((None, HEADS, tm, KPAD), lambda b, i: (b, 0, i, 0)),
            pl.BlockSpec((None, nt, HEADS * HEAD_V, T), lambda b, i: (b, i, 0, 0)),
        ],
        out_shape=[
            jax.ShapeDtypeStruct((B, S // T, HEADS, KPAD, T), BF16),
            jax.ShapeDtypeStruct((B, HEADS, S, KPAD), BF16),
            jax.ShapeDtypeStruct((B, S // T, HEADS * HEAD_V, T), BF16),
        ],
        scratch_shapes=[pltpu.VMEM((1, KPAD), F32)],
        compiler_params=_params(("parallel", "arbitrary")),
        name="fox_in",
    )(x, gmix, wqT, wkp, wvT, wf, bf, tri, e3)


def _attn_kernel(qT_ref, k_ref, vT_ref, oT_ref, *, nq, chunk):
    kpos = lax.broadcasted_iota(jnp.int32, (T, T), 0)
    qpos = lax.broadcasted_iota(jnp.int32, (T, T), 1)
    if chunk > 1:
        diag_mask = (kpos // chunk) <= (qpos // chunk)
    else:
        diag_mask = kpos <= qpos

    def q_body(qi, _):
        qT = qT_ref[qi]

        def kv_step(j, carry, masked):
            m, l, acc = carry
            k = k_ref[pl.ds(pl.multiple_of(j * T, T), T), :]
            s = _dot(k, qT)
            if masked:
                s = jnp.where(diag_mask, s, NEG_BIG)
            m_new = jnp.maximum(m, jnp.max(s, axis=0, keepdims=True))
            alpha = jnp.exp2(m - m_new)
            p = jnp.exp2(s - m_new)
            l = alpha * l + jnp.sum(p, axis=0, keepdims=True)
            acc = alpha * acc + _dot(vT_ref[j], p.astype(BF16))
            return m_new, l, acc

        init = (jnp.full((1, T), NEG_BIG, F32), jnp.zeros((1, T), F32), jnp.zeros((HEAD_V, T), F32))
        carry = lax.fori_loop(0, qi, lambda j, c: kv_step(j, c, False), init)
        _, l, acc = kv_step(qi, carry, True)
        oT_ref[qi] = (acc / l).astype(BF16)
        return 0

    lax.fori_loop(0, nq, q_body, 0)


def _attention(qT, k, vT, chunk):
    B, nq = qT.shape[0], qT.shape[1]
    S = nq * T
    kern = functools.partial(_attn_kernel, nq=nq, chunk=chunk)
    return pl.pallas_call(
        kern,
        grid=(B, HEADS),
        in_specs=[
            pl.BlockSpec((None, nq, None, KPAD, T), lambda b, h: (b, 0, h, 0, 0)),
            pl.BlockSpec((None, None, S, KPAD), lambda b, h: (b, h, 0, 0)),
            pl.BlockSpec((None, nq, HEAD_V, T), lambda b, h: (b, 0, h, 0)),
        ],
        out_specs=pl.BlockSpec((None, nq, HEAD_V, T), lambda b, h: (b, 0, h, 0)),
        out_shape=jax.ShapeDtypeStruct((B, nq, HEADS * HEAD_V, T), BF16),
        compiler_params=_params(("parallel", "parallel")),
        name="attn_chunk%d" % chunk,
    )(qT, k, vT)


def _glu_kernel(x_ref, gmix_ref, wc_ref, u_ref):
    h = _rms(x_ref[...], gmix_ref[...]).astype(BF16)
    ci = _dot(h, wc_ref[...])
    u_ref[...] = ci[:, :CONV_CHANNELS] * jax.nn.sigmoid(ci[:, CONV_CHANNELS:])


def _glu(x, gmix, wc, tm):
    B, S, D = x.shape
    return pl.pallas_call(
        _glu_kernel,
        grid=(B, S // tm),
        in_specs=[pl.BlockSpec((None, tm, D), lambda b, i: (b, i, 0)),
                  _const_spec(gmix.shape), _const_spec(wc.shape)],
        out_specs=pl.BlockSpec((None, tm, CONV_CHANNELS), lambda b, i: (b, i, 0)),
        out_shape=jax.ShapeDtypeStruct((B, S, CONV_CHANNELS), F32),
        compiler_params=_params(("parallel", "parallel")),
        name="glu",
    )(x, gmix, wc)


def _conv_kernel(u_ref, uprev_ref, w_ref, b_ref, lng_ref, lnb_ref, o_ref, buf_ref, *, ts, rows):
    first = pl.program_id(1) == 0
    buf_ref[0:HALO, :] = jnp.where(first, 0.0, uprev_ref[...])
    buf_ref[HALO:HALO + ts, :] = u_ref[...]
    off = HALO - (CONV_WIDTH - 1)
    w = w_ref[...]
    for c in range(ts // rows):
        acc = jnp.zeros((rows, CONV_CHANNELS), F32)
        for kk in range(CONV_WIDTH):
            acc = acc + buf_ref[c * rows + off + kk:c * rows + off + kk + rows, :] * w[kk:kk + 1, :]
        y = acc + b_ref[...]
        mu = jnp.mean(y, axis=-1, keepdims=True)
        d = y - mu
        var = jnp.mean(d * d, axis=-1, keepdims=True)
        z = d * lax.rsqrt(var + LN_EPS) * lng_ref[...] + lnb_ref[...]
        o_ref[c * rows:(c + 1) * rows, :] = (z * jax.nn.sigmoid(z)).astype(BF16)


def _conv(u, w, b, lng, lnb, ts):
    B, S, C = u.shape
    rows = 64
    kern = functools.partial(_conv_kernel, ts=ts, rows=rows)
    per = ts // HALO
    return pl.pallas_call(
        kern,
        grid=(B, S // ts),
        in_specs=[
            pl.BlockSpec((None, ts, C), lambda b, i: (b, i, 0)),
            pl.BlockSpec((None, HALO, C), lambda b, i: (b, jnp.maximum(i * per - 1, 0), 0)),
            _const_spec(w.shape), _const_spec(b.shape), _const_spec(lng.shape), _const_spec(lnb.shape),
        ],
        out_specs=pl.BlockSpec((None, ts, C), lambda b, i: (b, i, 0)),
        out_shape=jax.ShapeDtypeStruct((B, S, C), BF16),
        scratch_shapes=[pltpu.VMEM((HALO + ts, C), F32)],
        compiler_params=_params(("parallel", "parallel")),
        name="conv",
    )(u, u, w, b, lng, lnb)


def _branch_out_kernel(x_ref, gmix_ref, wg_ref, bg_ref, oaT_ref, obT_ref, uc_ref,
                       wa_ref, wb_ref, wc_ref, wo_ref, o_ref):
    x = x_ref[...]
    D = x.shape[-1]
    h = _rms(x, gmix_ref[...]).astype(BF16)
    g = jax.nn.sigmoid(_dot(h, wg_ref[...]) + bg_ref[...])
    y = g[:, 0:D] * _dot_tn(oaT_ref[...], wa_ref[...])
    y = y + g[:, D:2 * D] * _dot_tn(obT_ref[...], wb_ref[...])
    y = y + g[:, 2 * D:3 * D] * _dot(uc_ref[...], wc_ref[...])
    o_ref[...] = x + _dot(y.astype(BF16), wo_ref[...])


def _branch_out(x, gmix, wg, bg, oaT, obT, uc, wa, wb, wc, wo):
    B, S, D = x.shape
    C = HEADS * HEAD_V
    return pl.pallas_call(
        _branch_out_kernel,
        grid=(B, S // T),
        in_specs=[
            pl.BlockSpec((None, T, D), lambda b, i: (b, i, 0)),
            _const_spec(gmix.shape), _const_spec(wg.shape), _const_spec(bg.shape),
            pl.BlockSpec((None, None, C, T), lambda b, i: (b, i, 0, 0)),
            pl.BlockSpec((None, None, C, T), lambda b, i: (b, i, 0, 0)),
            pl.BlockSpec((None, T, CONV_CHANNELS), lambda b, i: (b, i, 0)),
            _const_spec(wa.shape), _const_spec(wb.shape), _const_spec(wc.shape), _const_spec(wo.shape),
        ],
        out_specs=pl.BlockSpec((None, T, D), lambda b, i: (b, i, 0)),
        out_shape=jax.ShapeDtypeStruct((B, S, D), F32),
        compiler_params=_params(("parallel", "parallel")),
        name="branch_out",
    )(x, gmix, wg, bg, oaT, obT, uc, wa, wb, wc, wo)


def _ffn_kernel(x_ref, g_ref, wg_ref, wu_ref, wd_ref, o_ref):
    x = x_ref[...]
    h = _rms(x, g_ref[...]).astype(BF16)
    a = _dot(h, wg_ref[...])
    u = _dot(h, wu_ref[...])
    ff = (a * jax.nn.sigmoid(a) * u).astype(BF16)
    o_ref[...] = x + _dot(ff, wd_ref[...])


def _ffn(x, g, wg, wu, wd, tm):
    B, S, D = x.shape
    return pl.pallas_call(
        _ffn_kernel,
        grid=(B, S // tm),
        in_specs=[pl.BlockSpec((None, tm, D), lambda b, i: (b, i, 0)),
                  _const_spec(g.shape), _const_spec(wg.shape), _const_spec(wu.shape), _const_spec(wd.shape)],
        out_specs=pl.BlockSpec((None, tm, D), lambda b, i: (b, i, 0)),
        out_shape=jax.ShapeDtypeStruct((B, S, D), F32),
        compiler_params=_params(("parallel", "parallel")),
        name="ffn",
    )(x, g, wg, wu, wd)


def _final_norm_kernel(x_ref, g_ref, o_ref):
    o_ref[...] = _rms(x_ref[...], g_ref[...])


def _final_norm(x, g, tm):
    B, S, D = x.shape
    return pl.pallas_call(
        _final_norm_kernel,
        grid=(B, S // tm),
        in_specs=[pl.BlockSpec((None, tm, D), lambda b, i: (b, i, 0)), _const_spec(g.shape)],
        out_specs=pl.BlockSpec((None, tm, D), lambda b, i: (b, i, 0)),
        out_shape=jax.ShapeDtypeStruct((B, S, D), F32),
        compiler_params=_params(("parallel", "parallel")),
        name="final_norm",
    )(x, g)


def _prep_weights(w_in, w_uq, w_ukv, b_forget):
    L, D, _ = w_in.shape
    sizes = (Q_LORA, KV_LORA, MLA_ROPE, FOX_HEADS * FOX_HEAD_DIM, FOX_HEADS * FOX_HEAD_DIM,
             FOX_HEADS * FOX_HEAD_DIM, FOX_HEADS, 2 * CONV_CHANNELS, N_BRANCHES * D)
    offs = [0]
    for sz in sizes:
        offs.append(offs[-1] + sz)
    w_cq, w_ckv, w_kr, w_qb, w_kb, w_vb, w_f, w_conv, w_gate = [
        w_in[:, :, offs[i]:offs[i + 1]] for i in range(len(sizes))]
    half = MLA_ROPE // 2
    z = lambda n: jnp.zeros((L, D, n), w_in.dtype)
    kr_a = jnp.concatenate([z(MLA_NOPE), w_kr, z(KPAD - MLA_NOPE - MLA_ROPE)], axis=2)
    kr_b = jnp.concatenate([z(MLA_NOPE), w_kr[:, :, half:], w_kr[:, :, :half],
                            z(KPAD - MLA_NOPE - MLA_ROPE)], axis=2)
    wa = jnp.concatenate([w_cq, w_ckv, kr_a, kr_b], axis=2).astype(BF16)

    dq = MLA_NOPE + MLA_ROPE
    wq = w_uq.reshape(L, Q_LORA, MLA_HEADS, dq)
    wq = jnp.pad(wq, ((0, 0), (0, 0), (0, 0), (0, KPAD - dq)))
    wqT = wq.transpose(0, 2, 3, 1).reshape(L, MLA_HEADS * KPAD, Q_LORA).astype(BF16)
    wkv = w_ukv.reshape(L, KV_LORA, MLA_HEADS, MLA_NOPE + MLA_V)
    wkn = jnp.pad(wkv[..., :MLA_NOPE], ((0, 0), (0, 0), (0, 0), (0, KPAD - MLA_NOPE)))
    wkn = wkn.reshape(L, KV_LORA, MLA_HEADS * KPAD).astype(BF16)
    wvT = wkv[..., MLA_NOPE:].transpose(0, 2, 3, 1).reshape(L, MLA_HEADS * MLA_V, KV_LORA).astype(BF16)

    wqbT = w_qb.transpose(0, 2, 1).astype(BF16)
    wvbT = w_vb.transpose(0, 2, 1).astype(BF16)
    wkb = w_kb.reshape(L, D, FOX_HEADS, FOX_HEAD_DIM)
    wkb = jnp.pad(wkb, ((0, 0), (0, 0), (0, 0), (0, KPAD - FOX_HEAD_DIM)))
    wkb = wkb.reshape(L, D, FOX_HEADS * KPAD).astype(BF16)
    wf = jnp.pad(w_f, ((0, 0), (0, 0), (0, KPAD - FOX_HEADS))).astype(BF16)
    bf = jnp.pad(b_forget.astype(F32), ((0, 0), (0, KPAD - FOX_HEADS))).reshape(L, 1, KPAD)
    return dict(wa=wa, wqT=wqT, wkn=wkn, wvT=wvT, wqbT=wqbT, wkb=wkb, wvbT=wvbT, wf=wf, bf=bf,
                wconv=w_conv.astype(BF16), wgate=w_gate.astype(BF16))


def kernel(x, positions, norm_mix_g, w_in, b_gate, q_norm_g, w_uq, kv_norm_g, w_ukv, b_forget,
           dw_kernel, dw_bias, conv_ln_g, conv_ln_b, w_bo_a, w_bo_b, w_bo_c, w_out, norm_ffn_g,
           w_ffn_gate, w_ffn_up, w_ffn_down, final_norm_g):
    B, S, D = x.shape
    L = w_in.shape[0]
    assert S % T == 0 and D == 1024
    tm = T
    row = lambda a: a.reshape(L, 1, -1).astype(F32)
    pw = _prep_weights(w_in, w_uq, w_ukv, b_forget)
    gmix, bg, qg, kvg = row(norm_mix_g), row(b_gate), row(q_norm_g), row(kv_norm_g)
    dwb, lng, lnb, gffn = row(dw_bias), row(conv_ln_g), row(conv_ln_b), row(norm_ffn_g)
    woa, wob, woc, wo = (w.astype(BF16) for w in (w_bo_a, w_bo_b, w_bo_c, w_out))
    wfg, wfu, wfd = (w.astype(BF16) for w in (w_ffn_gate, w_ffn_up, w_ffn_down))

    cosT, sinT, cos128, sin128 = _rope_tables(positions)
    for l in range(L):
        qaT, ka, vaT = _mla_in(x, gmix[l], pw["wa"][l], qg[l], kvg[l], pw["wqT"][l], pw["wkn"][l],
                               pw["wvT"][l], cosT, sinT, cos128, sin128, tm)
        oaT = _attention(qaT, ka, vaT, CHUNK)
        qbT, kb, vbT = _fox_in(x, gmix[l], pw["wqbT"][l], pw["wkb"][l], pw["wvbT"][l],
                               pw["wf"][l], pw["bf"][l], tm)
        obT = _attention(qbT, kb, vbT, 1)
        u = _glu(x, gmix[l], pw["wconv"][l], tm)
        uc = _conv(u, dw_kernel[l].astype(F32), dwb[l], lng[l], lnb[l], tm)
        x = _branch_out(x, gmix[l], pw["wgate"][l], bg[l], oaT, obT, uc, woa[l], wob[l], woc[l], wo[l])
        x = _ffn(x, gffn[l], wfg[l], wfu[l], wfd[l], tm)
    return _final_norm(x, final_norm_g.reshape(1, D).astype(F32), tm)
```

```python
import functools
import math

import jax
import jax.numpy as jnp
from jax import lax
from jax.experimental import pallas as pl
from jax.experimental.pallas import tpu as pltpu

F32 = jnp.float32
BF16 = jnp.bfloat16

MLA_HEADS = 8
MLA_NOPE = 64
MLA_ROPE = 32
MLA_V = 64
Q_LORA = 384
KV_LORA = 256
ROPE_THETA = 10000.0
FOX_HEADS = 8
FOX_HEAD_DIM = 64
CONV_CHANNELS = 512
CONV_WIDTH = 31
CHUNK = 64
N_BRANCHES = 3
RMS_EPS = 1e-6
LN_EPS = 1e-5

HEADS = 8
HEAD_V = 64
VROWS = 80
KPAD = 128
T = 256
TQ = 512
G = 2
HALO = 32
LOG2E = math.log2(math.e)
NEG_BIG = -1e30
VMEM_LIMIT = 56 * 1024 * 1024


def _rms(x, g):
    ms = jnp.mean(x * x, axis=-1, keepdims=True)
    return x * lax.rsqrt(ms + RMS_EPS) * g


def _split3(v):
    hi = v.astype(BF16)
    r = v - hi.astype(F32)
    mid = r.astype(BF16)
    lo = (r - mid.astype(F32)).astype(BF16)
    return hi, mid, lo


def _dot(a, b):
    return jnp.dot(a, b, preferred_element_type=F32)


def _dot_nt(a, b):
    return lax.dot_general(a, b, (((1,), (1,)), ((), ())), preferred_element_type=F32)


def _dot_tn(a, b):
    return lax.dot_general(a, b, (((0,), (0,)), ((), ())), preferred_element_type=F32)


def _store_values(vT_ref, vT, tm):
    vT = vT.astype(BF16).reshape(HEADS, HEAD_V, tm)
    row = lax.broadcasted_iota(jnp.int32, (HEADS, VROWS - HEAD_V, T), 1)
    ones_row = jnp.where(row < 1, 1.0, 0.0).astype(BF16)
    for t in range(tm // T):
        vT_ref[t, :, 0:HEAD_V, :] = vT[:, :, t * T:(t + 1) * T]
        vT_ref[t, :, HEAD_V:, :] = ones_row


def _params(sem):
    return pltpu.CompilerParams(dimension_semantics=sem, vmem_limit_bytes=VMEM_LIMIT)


def _const_spec(shape):
    n = len(shape)
    return pl.BlockSpec(shape, lambda *_: (0,) * n)


def _rope_kernel(prow_ref, pcol_ref, invc_ref, invl_ref, sgn_ref, cosT_ref, sinT_ref, cos_ref, sin_ref):
    angT = prow_ref[...].astype(F32) * invc_ref[...]
    cosT_ref[...] = jnp.cos(angT)
    sinT_ref[...] = jnp.sin(angT)
    ang = pcol_ref[...].astype(F32) * invl_ref[...]
    sgn = sgn_ref[...]
    cos_ref[...] = jnp.cos(ang) * jnp.abs(sgn)
    sin_ref[...] = jnp.sin(ang) * sgn


def _rope_tables(positions):
    B, S = positions.shape
    half = MLA_ROPE // 2
    inv_freq = 1.0 / (ROPE_THETA ** (jnp.arange(0, MLA_ROPE, 2, dtype=F32) / MLA_ROPE))
    inv_col = inv_freq.reshape(half, 1)
    zeros = jnp.zeros((MLA_NOPE,), F32)
    tail = jnp.zeros((KPAD - MLA_NOPE - MLA_ROPE,), F32)
    inv_lane = jnp.concatenate([zeros, inv_freq, inv_freq, tail]).reshape(1, KPAD)
    ones = jnp.ones((half,), F32)
    sgn_lane = jnp.concatenate([zeros, -ones, ones, tail]).reshape(1, KPAD)
    TS = min(S, 1024)
    return pl.pallas_call(
        _rope_kernel,
        grid=(B, S // TS),
        in_specs=[
            pl.BlockSpec((None, 1, TS), lambda b, i: (b, 0, i)),
            pl.BlockSpec((None, TS, 1), lambda b, i: (b, i, 0)),
            _const_spec((half, 1)),
            _const_spec((1, KPAD)),
            _const_spec((1, KPAD)),
        ],
        out_specs=[
            pl.BlockSpec((None, half, TS), lambda b, i: (b, 0, i)),
            pl.BlockSpec((None, half, TS), lambda b, i: (b, 0, i)),
            pl.BlockSpec((None, TS, KPAD), lambda b, i: (b, i, 0)),
            pl.BlockSpec((None, TS, KPAD), lambda b, i: (b, i, 0)),
        ],
        out_shape=[
            jax.ShapeDtypeStruct((B, half, S), F32),
            jax.ShapeDtypeStruct((B, half, S), F32),
            jax.ShapeDtypeStruct((B, S, KPAD), F32),
            jax.ShapeDtypeStruct((B, S, KPAD), F32),
        ],
        compiler_params=_params(("parallel", "parallel")),
        name="rope_tables",
    )(positions.reshape(B, 1, S), positions.reshape(B, S, 1), inv_col, inv_lane, sgn_lane)


def _mla_in_kernel(x_ref, gmix_ref, wa_ref, qg_ref, kvg_ref, wqT_ref, wkn_ref, wvT_ref,
                   cosT_ref, sinT_ref, cos_ref, sin_ref, qT_ref, k_ref, vT_ref, *, tm):
    h = _rms(x_ref[...], gmix_ref[...]).astype(BF16)
    pa = _dot(h, wa_ref[...])
    cqn = _rms(pa[:, 0:Q_LORA], qg_ref[...]).astype(BF16)
    ckvn = _rms(pa[:, Q_LORA:Q_LORA + KV_LORA], kvg_ref[...]).astype(BF16)
    o = Q_LORA + KV_LORA
    kra = pa[:, o:o + KPAD]
    krb = pa[:, o + KPAD:o + 2 * KPAD]

    qT = _dot_nt(wqT_ref[...], cqn) * (LOG2E * (MLA_NOPE + MLA_ROPE) ** -0.5)
    qT = qT.reshape(HEADS, KPAD, tm)
    half = MLA_ROPE // 2
    x1 = qT[:, MLA_NOPE:MLA_NOPE + half, :]
    x2 = qT[:, MLA_NOPE + half:MLA_NOPE + 2 * half, :]
    c = cosT_ref[...][None]
    s = sinT_ref[...][None]
    qT = jnp.concatenate(
        [qT[:, :MLA_NOPE, :], x1 * c - x2 * s, x2 * c + x1 * s, qT[:, MLA_NOPE + 2 * half:, :]],
        axis=1).astype(BF16)
    for t in range(tm // TQ):
        qT_ref[t] = qT[:, :, t * TQ:(t + 1) * TQ]
    _store_values(vT_ref, _dot_nt(wvT_ref[...], ckvn), tm)

    krope = kra * cos_ref[...] + krb * sin_ref[...]
    kn = _dot(ckvn, wkn_ref[...])
    for hd in range(HEADS):
        k_ref[hd] = (kn[:, hd * KPAD:(hd + 1) * KPAD] + krope).astype(BF16)


def _mla_in(x, gmix, wa, qg, kvg, wqT, wkn, wvT, cosT, sinT, cos128, sin128, tm):
    B, S, D = x.shape
    nt = tm // T
    kern = functools.partial(_mla_in_kernel, tm=tm)
    half = MLA_ROPE // 2
    return pl.pallas_call(
        kern,
        grid=(B, S // tm),
        in_specs=[
            pl.BlockSpec((None, tm, D), lambda b, i: (b, i, 0)),
            _const_spec(gmix.shape), _const_spec(wa.shape), _const_spec(qg.shape),
            _const_spec(kvg.shape), _const_spec(wqT.shape), _const_spec(wkn.shape),
            _const_spec(wvT.shape),
            pl.BlockSpec((None, half, tm), lambda b, i: (b, 0, i)),
            pl.BlockSpec((None, half, tm), lambda b, i: (b, 0, i)),
            pl.BlockSpec((None, tm, KPAD), lambda b, i: (b, i, 0)),
            pl.BlockSpec((None, tm, KPAD), lambda b, i: (b, i, 0)),
        ],
        out_specs=[
            pl.BlockSpec((None, tm // TQ, HEADS, KPAD, TQ), lambda b, i: (b, i, 0, 0, 0)),
            pl.BlockSpec((None, HEADS, tm, KPAD), lambda b, i: (b, 0, i, 0)),
            pl.BlockSpec((None, nt, HEADS, VROWS, T), lambda b, i: (b, i, 0, 0, 0)),
        ],
        out_shape=[
            jax.ShapeDtypeStruct((B, S // TQ, HEADS, KPAD, TQ), BF16),
            jax.ShapeDtypeStruct((B, HEADS, S, KPAD), BF16),
            jax.ShapeDtypeStruct((B, S // T, HEADS, VROWS, T), BF16),
        ],
        compiler_params=_params(("parallel", "parallel")),
        name="mla_in",
    )(x, gmix, wa, qg, kvg, wqT, wkn, wvT, cosT, sinT, cos128, sin128)


def _fox_in_kernel(x_ref, gmix_ref, wqT_ref, wkp_ref, wvT_ref, wf_ref, bf_ref, tri_ref, e3_ref,
                   qT_ref, k_ref, vT_ref, carry_ref, *, tm):
    @pl.when(pl.program_id(1) == 0)
    def _():
        carry_ref[...] = jnp.zeros_like(carry_ref)

    h = _rms(x_ref[...], gmix_ref[...]).astype(BF16)
    qT = (_dot_nt(wqT_ref[...], h) * (LOG2E * FOX_HEAD_DIM ** -0.5)).astype(BF16)
    qT = qT.reshape(HEADS, FOX_HEAD_DIM, tm)
    row = lax.broadcasted_iota(jnp.int32, (HEADS, 16, TQ), 1)
    ones_rows = jnp.where(row < 3, 1.0, 0.0).astype(BF16)
    zero_rows = jnp.zeros((HEADS, KPAD - FOX_HEAD_DIM - 16, TQ), BF16)
    for t in range(tm // TQ):
        qT_ref[t, :, 0:FOX_HEAD_DIM, :] = qT[:, :, t * TQ:(t + 1) * TQ]
        qT_ref[t, :, FOX_HEAD_DIM:FOX_HEAD_DIM + 16, :] = ones_rows
        qT_ref[t, :, FOX_HEAD_DIM + 16:, :] = zero_rows
    _store_values(vT_ref, _dot_nt(wvT_ref[...], h), tm)

    f = _dot(h, wf_ref[...]) + bf_ref[...]
    lane = lax.broadcasted_iota(jnp.int32, f.shape, 1)
    logf = jnp.where(lane < FOX_HEADS, jax.nn.log_sigmoid(f), 0.0)
    hi, mid, lo = _split3(logf)
    tri = tri_ref[...]
    cum = _dot(tri, hi) + _dot(tri, mid) + _dot(tri, lo) + carry_ref[...]
    carry_ref[...] = cum[tm - 1:tm, :]
    hi, mid, lo = _split3(cum * (-LOG2E))
    place = _dot(jnp.concatenate([hi, mid, lo], axis=1), e3_ref[...])
    kp = _dot(h, wkp_ref[...]) + place
    for hd in range(HEADS):
        k_ref[hd] = kp[:, hd * KPAD:(hd + 1) * KPAD].astype(BF16)


def _fox_in(x, gmix, wqT, wkp, wvT, wf, bf, tm):
    B, S, D = x.shape
    nt = tm // T
    r = lax.broadcasted_iota(jnp.int32, (tm, tm), 0)
    c = lax.broadcasted_iota(jnp.int32, (tm, tm), 1)
    tri = (c <= r).astype(BF16)
    rr = jnp.arange(3 * KPAD)
    cc = jnp.arange(HEADS * KPAD)
    part, hd = rr // KPAD, rr % KPAD
    e3 = ((cc[None, :] == (hd * KPAD + FOX_HEAD_DIM + part)[:, None]) & (hd < HEADS)[:, None]).astype(BF16)
    kern = functools.partial(_fox_in_kernel, tm=tm)
    return pl.pallas_call(
        kern,
        grid=(B, S // tm),
        in_specs=[
            pl.BlockSpec((None, tm, D), lambda b, i: (b, i, 0)),
            _const_spec(gmix.shape), _const_spec(wqT.shape), _const_spec(wkp.shape),
            _const_spec(wvT.shape), _const_spec(wf.shape), _const_spec(bf.shape),
            _const_spec(tri.shape), _const_spec(e3.shape),
        ],
        out_specs=[
            pl.BlockSpec((None, tm // TQ, HEADS, KPAD, TQ), lambda b, i: (b, i, 0, 0, 0)),
            pl.BlockSpec((None, HEADS, tm, KPAD), lambda b, i: (b, 0, i, 0)),
            pl.BlockSpec((None, nt, HEADS, VROWS, T), lambda b, i: (b, i, 0, 0, 0)),
        ],
        out_shape=[
            jax.ShapeDtypeStruct((B, S // TQ, HEADS, KPAD, TQ), BF16),
            jax.ShapeDtypeStruct((B, HEADS, S, KPAD), BF16),
            jax.ShapeDtypeStruct((B, S // T, HEADS, VROWS, T), BF16),
        ],
        scratch_shapes=[pltpu.VMEM((1, KPAD), F32)],
        compiler_params=_params(("parallel", "arbitrary")),
        name="fox_in",
    )(x, gmix, wqT, wkp, wvT, wf, bf, tri, e3)


def _attn_kernel(qT_ref, k_ref, vT_ref, oT_ref, acc_ref, s_ref, *, nqb, chunk):
    assert TQ == 2 * T
    kpos = lax.broadcasted_iota(jnp.int32, (T, TQ), 0)
    qpos = lax.broadcasted_iota(jnp.int32, (T, TQ), 1)

    def diag_mask(r):
        kp = kpos + r * T
        if chunk > 1:
            return (kp // chunk) <= (qpos // chunk)
        return kp <= qpos

    def q_body(qb, _):
        def scores(j, slot):
            for g in range(G):
                k = k_ref[g, pl.ds(pl.multiple_of(j * T, T), T), :]
                s_ref[slot, g] = _dot(k, qT_ref[qb, g])

        def process(j, slot, stats, mask):
            out = []
            for g in range(G):
                m = stats[g]
                s = s_ref[slot, g]
                if mask is not None:
                    s = jnp.where(mask, s, NEG_BIG)
                m_new = jnp.maximum(m, jnp.max(s, axis=0, keepdims=True))
                p = jnp.exp2(s - m_new).astype(BF16)
                pv = _dot(vT_ref[j, g], p)
                acc_ref[g] = jnp.exp2(m - m_new) * acc_ref[g] + pv
                out.append(m_new)
            return tuple(out)

        def pair(jj, stats):
            j = 2 * jj
            scores(j + 1, 1)
            stats = process(j, 0, stats, None)
            scores(j + 2, 0)
            return process(j + 1, 1, stats, None)

        acc_ref[...] = jnp.zeros_like(acc_ref)
        init = tuple(jnp.full((1, TQ), NEG_BIG, F32) for _ in range(G))
        scores(0, 0)
        stats = lax.fori_loop(0, qb, pair, init)
        j = 2 * qb
        scores(j + 1, 1)
        stats = process(j, 0, stats, diag_mask(0))
        stats = process(j + 1, 1, stats, diag_mask(1))
        for g in range(G):
            a = acc_ref[g]
            oT_ref[qb, g * HEAD_V:(g + 1) * HEAD_V, :] = (a[:HEAD_V] / a[HEAD_V:HEAD_V + 1]).astype(BF16)
        return 0

    lax.fori_loop(0, nqb, q_body, 0)


def _attention(qT, k, vT, chunk):
    B, nqb = qT.shape[0], qT.shape[1]
    S = nqb * TQ
    kern = functools.partial(_attn_kernel, nqb=nqb, chunk=chunk)
    return pl.pallas_call(
        kern,
        grid=(B, HEADS // G),
        in_specs=[
            pl.BlockSpec((None, nqb, G, KPAD, TQ), lambda b, h: (b, 0, h, 0, 0)),
            pl.BlockSpec((None, G, S, KPAD), lambda b, h: (b, h, 0, 0)),
            pl.BlockSpec((None, S // T, G, VROWS, T), lambda b, h: (b, 0, h, 0, 0)),
        ],
        out_specs=pl.BlockSpec((None, nqb, G * HEAD_V, TQ), lambda b, h: (b, 0, h, 0)),
        out_shape=jax.ShapeDtypeStruct((B, nqb, HEADS * HEAD_V, TQ), BF16),
        scratch_shapes=[pltpu.VMEM((G, VROWS, TQ), F32), pltpu.VMEM((2, G, T, TQ), F32)],
        compiler_params=_params(("parallel", "parallel")),
        name="attn_chunk%d" % chunk,
    )(qT, k, vT)


def _glu_kernel(x_ref, gmix_ref, wc_ref, u_ref):
    h = _rms(x_ref[...], gmix_ref[...]).astype(BF16)
    ci = _dot(h, wc_ref[...])
    u_ref[...] = ci[:, :CONV_CHANNELS] * jax.nn.sigmoid(ci[:, CONV_CHANNELS:])


def _glu(x, gmix, wc, tm):
    B, S, D = x.shape
    return pl.pallas_call(
        _glu_kernel,
        grid=(B, S // tm),
        in_specs=[pl.BlockSpec((None, tm, D), lambda b, i: (b, i, 0)),
                  _const_spec(gmix.shape), _const_spec(wc.shape)],
        out_specs=pl.BlockSpec((None, tm, CONV_CHANNELS), lambda b, i: (b, i, 0)),
        out_shape=jax.ShapeDtypeStruct((B, S, CONV_CHANNELS), F32),
        compiler_params=_params(("parallel", "parallel")),
        name="glu",
    )(x, gmix, wc)


def _conv_kernel(u_ref, uprev_ref, w_ref, b_ref, lng_ref, lnb_ref, o_ref, buf_ref, *, ts, rows):
    first = pl.program_id(1) == 0
    buf_ref[0:HALO, :] = jnp.where(first, 0.0, uprev_ref[...])
    buf_ref[HALO:HALO + ts, :] = u_ref[...]
    off = HALO - (CONV_WIDTH - 1)
    w = w_ref[...]
    for c in range(ts // rows):
        acc = jnp.zeros((rows, CONV_CHANNELS), F32)
        for kk in range(CONV_WIDTH):
            acc = acc + buf_ref[c * rows + off + kk:c * rows + off + kk + rows, :] * w[kk:kk + 1, :]
        y = acc + b_ref[...]
        mu = jnp.mean(y, axis=-1, keepdims=True)
        d = y - mu
        var = jnp.mean(d * d, axis=-1, keepdims=True)
        z = d * lax.rsqrt(var + LN_EPS) * lng_ref[...] + lnb_ref[...]
        o_ref[c * rows:(c + 1) * rows, :] = (z * jax.nn.sigmoid(z)).astype(BF16)


def _conv(u, w, b, lng, lnb, ts):
    B, S, C = u.shape
    rows = 64
    kern = functools.partial(_conv_kernel, ts=ts, rows=rows)
    per = ts // HALO
    return pl.pallas_call(
        kern,
        grid=(B, S // ts),
        in_specs=[
            pl.BlockSpec((None, ts, C), lambda b, i: (b, i, 0)),
            pl.BlockSpec((None, HALO, C), lambda b, i: (b, jnp.maximum(i * per - 1, 0), 0)),
            _const_spec(w.shape), _const_spec(b.shape), _const_spec(lng.shape), _const_spec(lnb.shape),
        ],
        out_specs=pl.BlockSpec((None, ts, C), lambda b, i: (b, i, 0)),
        out_shape=jax.ShapeDtypeStruct((B, S, C), BF16),
        scratch_shapes=[pltpu.VMEM((HALO + ts, C), F32)],
        compiler_params=_params(("parallel", "parallel")),
        name="conv",
    )(u, u, w, b, lng, lnb)


def _branch_out_kernel(x_ref, gmix_ref, wg_ref, bg_ref, oaT_ref, obT_ref, uc_ref,
                       wa_ref, wb_ref, wc_ref, wo_ref, o_ref):
    x = x_ref[...]
    D = x.shape[-1]
    h = _rms(x, gmix_ref[...]).astype(BF16)
    g = jax.nn.sigmoid(_dot(h, wg_ref[...]) + bg_ref[...])
    y = g[:, 0:D] * _dot_tn(oaT_ref[...], wa_ref[...])
    y = y + g[:, D:2 * D] * _dot_tn(obT_ref[...], wb_ref[...])
    y = y + g[:, 2 * D:3 * D] * _dot(uc_ref[...], wc_ref[...])
    o_ref[...] = x + _dot(y.astype(BF16), wo_ref[...])


def _branch_out(x, gmix, wg, bg, oaT, obT, uc, wa, wb, wc, wo):
    B, S, D = x.shape
    C = HEADS * HEAD_V
    return pl.pallas_call(
        _branch_out_kernel,
        grid=(B, S // TQ),
        in_specs=[
            pl.BlockSpec((None, TQ, D), lambda b, i: (b, i, 0)),
            _const_spec(gmix.shape), _const_spec(wg.shape), _const_spec(bg.shape),
            pl.BlockSpec((None, None, C, TQ), lambda b, i: (b, i, 0, 0)),
            pl.BlockSpec((None, None, C, TQ), lambda b, i: (b, i, 0, 0)),
            pl.BlockSpec((None, TQ, CONV_CHANNELS), lambda b, i: (b, i, 0)),
            _const_spec(wa.shape), _const_spec(wb.shape), _const_spec(wc.shape), _const_spec(wo.shape),
        ],
        out_specs=pl.BlockSpec((None, TQ, D), lambda b, i: (b, i, 0)),
        out_shape=jax.ShapeDtypeStruct((B, S, D), F32),
        compiler_params=_params(("parallel", "parallel")),
        name="branch_out",
    )(x, gmix, wg, bg, oaT, obT, uc, wa, wb, wc, wo)


def _ffn_kernel(x_ref, g_ref, wg_ref, wu_ref, wd_ref, o_ref):
    x = x_ref[...]
    h = _rms(x, g_ref[...]).astype(BF16)
    a = _dot(h, wg_ref[...])
    u = _dot(h, wu_ref[...])
    ff = (a * jax.nn.sigmoid(a) * u).astype(BF16)
    o_ref[...] = x + _dot(ff, wd_ref[...])


def _ffn(x, g, wg, wu, wd, tm):
    B, S, D = x.shape
    return pl.pallas_call(
        _ffn_kernel,
        grid=(B, S // tm),
        in_specs=[pl.BlockSpec((None, tm, D), lambda b, i: (b, i, 0)),
                  _const_spec(g.shape), _const_spec(wg.shape), _const_spec(wu.shape), _const_spec(wd.shape)],
        out_specs=pl.BlockSpec((None, tm, D), lambda b, i: (b, i, 0)),
        out_shape=jax.ShapeDtypeStruct((B, S, D), F32),
        compiler_params=_params(("parallel", "parallel")),
        name="ffn",
    )(x, g, wg, wu, wd)


def _final_norm_kernel(x_ref, g_ref, o_ref):
    o_ref[...] = _rms(x_ref[...], g_ref[...])


def _final_norm(x, g, tm):
    B, S, D = x.shape
    return pl.pallas_call(
        _final_norm_kernel,
        grid=(B, S // tm),
        in_specs=[pl.BlockSpec((None, tm, D), lambda b, i: (b, i, 0)), _const_spec(g.shape)],
        out_specs=pl.BlockSpec((None, tm, D), lambda b, i: (b, i, 0)),
        out_shape=jax.ShapeDtypeStruct((B, S, D), F32),
        compiler_params=_params(("parallel", "parallel")),
        name="final_norm",
    )(x, g)


def _prep_weights(w_in, w_uq, w_ukv, b_forget):
    L, D, _ = w_in.shape
    sizes = (Q_LORA, KV_LORA, MLA_ROPE, FOX_HEADS * FOX_HEAD_DIM, FOX_HEADS * FOX_HEAD_DIM,
             FOX_HEADS * FOX_HEAD_DIM, FOX_HEADS, 2 * CONV_CHANNELS, N_BRANCHES * D)
    offs = [0]
    for sz in sizes:
        offs.append(offs[-1] + sz)
    w_cq, w_ckv, w_kr, w_qb, w_kb, w_vb, w_f, w_conv, w_gate = [
        w_in[:, :, offs[i]:offs[i + 1]] for i in range(len(sizes))]
    half = MLA_ROPE // 2
    z = lambda n: jnp.zeros((L, D, n), w_in.dtype)
    kr_a = jnp.concatenate([z(MLA_NOPE), w_kr, z(KPAD - MLA_NOPE - MLA_ROPE)], axis=2)
    kr_b = jnp.concatenate([z(MLA_NOPE), w_kr[:, :, half:], w_kr[:, :, :half],
                            z(KPAD - MLA_NOPE - MLA_ROPE)], axis=2)
    wa = jnp.concatenate([w_cq, w_ckv, kr_a, kr_b], axis=2).astype(BF16)

    dq = MLA_NOPE + MLA_ROPE
    wq = w_uq.reshape(L, Q_LORA, MLA_HEADS, dq)
    wq = jnp.pad(wq, ((0, 0), (0, 0), (0, 0), (0, KPAD - dq)))
    wqT = wq.transpose(0, 2, 3, 1).reshape(L, MLA_HEADS * KPAD, Q_LORA).astype(BF16)
    wkv = w_ukv.reshape(L, KV_LORA, MLA_HEADS, MLA_NOPE + MLA_V)
    wkn = jnp.pad(wkv[..., :MLA_NOPE], ((0, 0), (0, 0), (0, 0), (0, KPAD - MLA_NOPE)))
    wkn = wkn.reshape(L, KV_LORA, MLA_HEADS * KPAD).astype(BF16)
    wvT = wkv[..., MLA_NOPE:].transpose(0, 2, 3, 1).reshape(L, MLA_HEADS * MLA_V, KV_LORA).astype(BF16)

    wqbT = w_qb.transpose(0, 2, 1).astype(BF16)
    wvbT = w_vb.transpose(0, 2, 1).astype(BF16)
    wkb = w_kb.reshape(L, D, FOX_HEADS, FOX_HEAD_DIM)
    wkb = jnp.pad(wkb, ((0, 0), (0, 0), (0, 0), (0, KPAD - FOX_HEAD_DIM)))
    wkb = wkb.reshape(L, D, FOX_HEADS * KPAD).astype(BF16)
    wf = jnp.pad(w_f, ((0, 0), (0, 0), (0, KPAD - FOX_HEADS))).astype(BF16)
    bf = jnp.pad(b_forget.astype(F32), ((0, 0), (0, KPAD - FOX_HEADS))).reshape(L, 1, KPAD)
    return dict(wa=wa, wqT=wqT, wkn=wkn, wvT=wvT, wqbT=wqbT, wkb=wkb, wvbT=wvbT, wf=wf, bf=bf,
                wconv=w_conv.astype(BF16), wgate=w_gate.astype(BF16))


def kernel(x, positions, norm_mix_g, w_in, b_gate, q_norm_g, w_uq, kv_norm_g, w_ukv, b_forget,
           dw_kernel, dw_bias, conv_ln_g, conv_ln_b, w_bo_a, w_bo_b, w_bo_c, w_out, norm_ffn_g,
           w_ffn_gate, w_ffn_up, w_ffn_down, final_norm_g):
    B, S, D = x.shape
    L = w_in.shape[0]
    assert S % TQ == 0 and D == 1024
    tm = TQ
    tf = T
    row = lambda a: a.reshape(L, 1, -1).astype(F32)
    pw = _prep_weights(w_in, w_uq, w_ukv, b_forget)
    gmix, bg, qg, kvg = row(norm_mix_g), row(b_gate), row(q_norm_g), row(kv_norm_g)
    dwb, lng, lnb, gffn = row(dw_bias), row(conv_ln_g), row(conv_ln_b), row(norm_ffn_g)
    woa, wob, woc, wo = (w.astype(BF16) for w in (w_bo_a, w_bo_b, w_bo_c, w_out))
    wfg, wfu, wfd = (w.astype(BF16) for w in (w_ffn_gate, w_ffn_up, w_ffn_down))

    cosT, sinT, cos128, sin128 = _rope_tables(positions)
    for l in range(L):
        qaT, ka, vaT = _mla_in(x, gmix[l], pw["wa"][l], qg[l], kvg[l], pw["wqT"][l], pw["wkn"][l],
                               pw["wvT"][l], cosT, sinT, cos128, sin128, tm)
        oaT = _attention(qaT, ka, vaT, CHUNK)
        qbT, kb, vbT = _fox_in(x, gmix[l], pw["wqbT"][l], pw["wkb"][l], pw["wvbT"][l],
                               pw["wf"][l], pw["bf"][l], tm)
        obT = _attention(qbT, kb, vbT, 1)
        u = _glu(x, gmix[l], pw["wconv"][l], tm)
        uc = _conv(u, dw_kernel[l].astype(F32), dwb[l], lng[l], lnb[l], tm)
        x = _branch_out(x, gmix[l], pw["wgate"][l], bg[l], oaT, obT, uc, woa[l], wob[l], woc[l], wo[l])
        x = _ffn(x, gffn[l], wfg[l], wfu[l], wfd[l], tf)
    return _final_norm(x, final_norm_g.reshape(1, D).astype(F32), tm)
```

```python
import functools
import math

import jax
import jax.numpy as jnp
from jax import lax
from jax.experimental import pallas as pl
from jax.experimental.pallas import tpu as pltpu

F32 = jnp.float32
BF16 = jnp.bfloat16

MLA_HEADS = 8
MLA_NOPE = 64
MLA_ROPE = 32
MLA_V = 64
Q_LORA = 384
KV_LORA = 256
ROPE_THETA = 10000.0
FOX_HEADS = 8
FOX_HEAD_DIM = 64
CONV_CHANNELS = 512
CONV_WIDTH = 31
CHUNK = 64
N_BRANCHES = 3
RMS_EPS = 1e-6
LN_EPS = 1e-5

HEADS = 8
HEAD_V = 64
VROWS = 80
KPAD = 128
T = 256
TQ = 512
G = 2
HALO = 32
LOG2E = math.log2(math.e)
NEG_BIG = -1e30
VMEM_LIMIT = 56 * 1024 * 1024


def _rms(x, g):
    ms = jnp.mean(x * x, axis=-1, keepdims=True)
    return x * lax.rsqrt(ms + RMS_EPS) * g


def _split3(v):
    hi = v.astype(BF16)
    r = v - hi.astype(F32)
    mid = r.astype(BF16)
    lo = (r - mid.astype(F32)).astype(BF16)
    return hi, mid, lo


def _dot(a, b):
    return jnp.dot(a, b, preferred_element_type=F32)


def _dot_nt(a, b):
    return lax.dot_general(a, b, (((1,), (1,)), ((), ())), preferred_element_type=F32)


def _dot_tn(a, b):
    return lax.dot_general(a, b, (((0,), (0,)), ((), ())), preferred_element_type=F32)


def _store_values(vT_ref, vT, tm):
    vT = vT.astype(BF16).reshape(HEADS, HEAD_V, tm)
    row = lax.broadcasted_iota(jnp.int32, (HEADS, VROWS - HEAD_V, T), 1)
    ones_row = jnp.where(row < 1, 1.0, 0.0).astype(BF16)
    for t in range(tm // T):
        vT_ref[t, :, 0:HEAD_V, :] = vT[:, :, t * T:(t + 1) * T]
        vT_ref[t, :, HEAD_V:, :] = ones_row


def _params(sem):
    return pltpu.CompilerParams(dimension_semantics=sem, vmem_limit_bytes=VMEM_LIMIT)


def _const_spec(shape):
    n = len(shape)
    return pl.BlockSpec(shape, lambda *_: (0,) * n)


def _rope_kernel(prow_ref, pcol_ref, invc_ref, invl_ref, sgn_ref, cosT_ref, sinT_ref, cos_ref, sin_ref):
    angT = prow_ref[...].astype(F32) * invc_ref[...]
    cosT_ref[...] = jnp.cos(angT)
    sinT_ref[...] = jnp.sin(angT)
    ang = pcol_ref[...].astype(F32) * invl_ref[...]
    sgn = sgn_ref[...]
    cos_ref[...] = jnp.cos(ang) * jnp.abs(sgn)
    sin_ref[...] = jnp.sin(ang) * sgn


def _rope_tables(positions):
    B, S = positions.shape
    half = MLA_ROPE // 2
    inv_freq = 1.0 / (ROPE_THETA ** (jnp.arange(0, MLA_ROPE, 2, dtype=F32) / MLA_ROPE))
    inv_col = inv_freq.reshape(half, 1)
    zeros = jnp.zeros((MLA_NOPE,), F32)
    tail = jnp.zeros((KPAD - MLA_NOPE - MLA_ROPE,), F32)
    inv_lane = jnp.concatenate([zeros, inv_freq, inv_freq, tail]).reshape(1, KPAD)
    ones = jnp.ones((half,), F32)
    sgn_lane = jnp.concatenate([zeros, -ones, ones, tail]).reshape(1, KPAD)
    TS = min(S, 1024)
    return pl.pallas_call(
        _rope_kernel,
        grid=(B, S // TS),
        in_specs=[
            pl.BlockSpec((None, 1, TS), lambda b, i: (b, 0, i)),
            pl.BlockSpec((None, TS, 1), lambda b, i: (b, i, 0)),
            _const_spec((half, 1)),
            _const_spec((1, KPAD)),
            _const_spec((1, KPAD)),
        ],
        out_specs=[
            pl.BlockSpec((None, half, TS), lambda b, i: (b, 0, i)),
            pl.BlockSpec((None, half, TS), lambda b, i: (b, 0, i)),
            pl.BlockSpec((None, TS, KPAD), lambda b, i: (b, i, 0)),
            pl.BlockSpec((None, TS, KPAD), lambda b, i: (b, i, 0)),
        ],
        out_shape=[
            jax.ShapeDtypeStruct((B, half, S), F32),
            jax.ShapeDtypeStruct((B, half, S), F32),
            jax.ShapeDtypeStruct((B, S, KPAD), F32),
            jax.ShapeDtypeStruct((B, S, KPAD), F32),
        ],
        compiler_params=_params(("parallel", "parallel")),
        name="rope_tables",
    )(positions.reshape(B, 1, S), positions.reshape(B, S, 1), inv_col, inv_lane, sgn_lane)


def _mla_in_kernel(x_ref, gmix_ref, wa_ref, qg_ref, kvg_ref, wqT_ref, wkn_ref, wvT_ref,
                   cosT_ref, sinT_ref, cos_ref, sin_ref, qT_ref, k_ref, vT_ref, *, tm):
    h = _rms(x_ref[...], gmix_ref[...]).astype(BF16)
    pa = _dot(h, wa_ref[...])
    cqn = _rms(pa[:, 0:Q_LORA], qg_ref[...]).astype(BF16)
    ckvn = _rms(pa[:, Q_LORA:Q_LORA + KV_LORA], kvg_ref[...]).astype(BF16)
    o = Q_LORA + KV_LORA
    kra = pa[:, o:o + KPAD]
    krb = pa[:, o + KPAD:o + 2 * KPAD]

    qT = _dot_nt(wqT_ref[...], cqn) * (LOG2E * (MLA_NOPE + MLA_ROPE) ** -0.5)
    qT = qT.reshape(HEADS, KPAD, tm)
    half = MLA_ROPE // 2
    x1 = qT[:, MLA_NOPE:MLA_NOPE + half, :]
    x2 = qT[:, MLA_NOPE + half:MLA_NOPE + 2 * half, :]
    c = cosT_ref[...][None]
    s = sinT_ref[...][None]
    qT = jnp.concatenate(
        [qT[:, :MLA_NOPE, :], x1 * c - x2 * s, x2 * c + x1 * s, qT[:, MLA_NOPE + 2 * half:, :]],
        axis=1).astype(BF16)
    for t in range(tm // TQ):
        qT_ref[t] = qT[:, :, t * TQ:(t + 1) * TQ]
    _store_values(vT_ref, _dot_nt(wvT_ref[...], ckvn), tm)

    krope = kra * cos_ref[...] + krb * sin_ref[...]
    kn = _dot(ckvn, wkn_ref[...])
    for hd in range(HEADS):
        k_ref[hd] = (kn[:, hd * KPAD:(hd + 1) * KPAD] + krope).astype(BF16)


def _mla_in(x, gmix, wa, qg, kvg, wqT, wkn, wvT, cosT, sinT, cos128, sin128, tm):
    B, S, D = x.shape
    nt = tm // T
    kern = functools.partial(_mla_in_kernel, tm=tm)
    half = MLA_ROPE // 2
    return pl.pallas_call(
        kern,
        grid=(B, S // tm),
        in_specs=[
            pl.BlockSpec((None, tm, D), lambda b, i: (b, i, 0)),
            _const_spec(gmix.shape), _const_spec(wa.shape), _const_spec(qg.shape),
            _const_spec(kvg.shape), _const_spec(wqT.shape), _const_spec(wkn.shape),
            _const_spec(wvT.shape),
            pl.BlockSpec((None, half, tm), lambda b, i: (b, 0, i)),
            pl.BlockSpec((None, half, tm), lambda b, i: (b, 0, i)),
            pl.BlockSpec((None, tm, KPAD), lambda b, i: (b, i, 0)),
            pl.BlockSpec((None, tm, KPAD), lambda b, i: (b, i, 0)),
        ],
        out_specs=[
            pl.BlockSpec((None, tm // TQ, HEADS, KPAD, TQ), lambda b, i: (b, i, 0, 0, 0)),
            pl.BlockSpec((None, HEADS, tm, KPAD), lambda b, i: (b, 0, i, 0)),
            pl.BlockSpec((None, nt, HEADS, VROWS, T), lambda b, i: (b, i, 0, 0, 0)),
        ],
        out_shape=[
            jax.ShapeDtypeStruct((B, S // TQ, HEADS, KPAD, TQ), BF16),
            jax.ShapeDtypeStruct((B, HEADS, S, KPAD), BF16),
            jax.ShapeDtypeStruct((B, S // T, HEADS, VROWS, T), BF16),
        ],
        compiler_params=_params(("parallel", "parallel")),
        name="mla_in",
    )(x, gmix, wa, qg, kvg, wqT, wkn, wvT, cosT, sinT, cos128, sin128)


def _fox_in_kernel(x_ref, gmix_ref, wqT_ref, wkp_ref, wvT_ref, wf_ref, bf_ref, tri_ref, e3_ref,
                   qT_ref, k_ref, vT_ref, carry_ref, *, tm):
    @pl.when(pl.program_id(1) == 0)
    def _():
        carry_ref[...] = jnp.zeros_like(carry_ref)

    h = _rms(x_ref[...], gmix_ref[...]).astype(BF16)
    qT = (_dot_nt(wqT_ref[...], h) * (LOG2E * FOX_HEAD_DIM ** -0.5)).astype(BF16)
    qT = qT.reshape(HEADS, FOX_HEAD_DIM, tm)
    row = lax.broadcasted_iota(jnp.int32, (HEADS, 16, TQ), 1)
    ones_rows = jnp.where(row < 3, 1.0, 0.0).astype(BF16)
    zero_rows = jnp.zeros((HEADS, KPAD - FOX_HEAD_DIM - 16, TQ), BF16)
    for t in range(tm // TQ):
        qT_ref[t, :, 0:FOX_HEAD_DIM, :] = qT[:, :, t * TQ:(t + 1) * TQ]
        qT_ref[t, :, FOX_HEAD_DIM:FOX_HEAD_DIM + 16, :] = ones_rows
        qT_ref[t, :, FOX_HEAD_DIM + 16:, :] = zero_rows
    _store_values(vT_ref, _dot_nt(wvT_ref[...], h), tm)

    f = _dot(h, wf_ref[...]) + bf_ref[...]
    lane = lax.broadcasted_iota(jnp.int32, f.shape, 1)
    logf = jnp.where(lane < FOX_HEADS, jax.nn.log_sigmoid(f), 0.0)
    hi, mid, lo = _split3(logf)
    tri = tri_ref[...]
    cum = _dot(tri, hi) + _dot(tri, mid) + _dot(tri, lo) + carry_ref[...]
    carry_ref[...] = cum[tm - 1:tm, :]
    hi, mid, lo = _split3(cum * (-LOG2E))
    place = _dot(jnp.concatenate([hi, mid, lo], axis=1), e3_ref[...])
    kp = _dot(h, wkp_ref[...]) + place
    for hd in range(HEADS):
        k_ref[hd] = kp[:, hd * KPAD:(hd + 1) * KPAD].astype(BF16)


def _fox_in(x, gmix, wqT, wkp, wvT, wf, bf, tm):
    B, S, D = x.shape
    nt = tm // T
    r = lax.broadcasted_iota(jnp.int32, (tm, tm), 0)
    c = lax.broadcasted_iota(jnp.int32, (tm, tm), 1)
    tri = (c <= r).astype(BF16)
    rr = jnp.arange(3 * KPAD)
    cc = jnp.arange(HEADS * KPAD)
    part, hd = rr // KPAD, rr % KPAD
    e3 = ((cc[None, :] == (hd * KPAD + FOX_HEAD_DIM + part)[:, None]) & (hd < HEADS)[:, None]).astype(BF16)
    kern = functools.partial(_fox_in_kernel, tm=tm)
    return pl.pallas_call(
        kern,
        grid=(B, S // tm),
        in_specs=[
            pl.BlockSpec((None, tm, D), lambda b, i: (b, i, 0)),
            _const_spec(gmix.shape), _const_spec(wqT.shape), _const_spec(wkp.shape),
            _const_spec(wvT.shape), _const_spec(wf.shape), _const_spec(bf.shape),
            _const_spec(tri.shape), _const_spec(e3.shape),
        ],
        out_specs=[
            pl.BlockSpec((None, tm // TQ, HEADS, KPAD, TQ), lambda b, i: (b, i, 0, 0, 0)),
            pl.BlockSpec((None, HEADS, tm, KPAD), lambda b, i: (b, 0, i, 0)),
            pl.BlockSpec((None, nt, HEADS, VROWS, T), lambda b, i: (b, i, 0, 0, 0)),
        ],
        out_shape=[
            jax.ShapeDtypeStruct((B, S // TQ, HEADS, KPAD, TQ), BF16),
            jax.ShapeDtypeStruct((B, HEADS, S, KPAD), BF16),
            jax.ShapeDtypeStruct((B, S // T, HEADS, VROWS, T), BF16),
        ],
        scratch_shapes=[pltpu.VMEM((1, KPAD), F32)],
        compiler_params=_params(("parallel", "arbitrary")),
        name="fox_in",
    )(x, gmix, wqT, wkp, wvT, wf, bf, tri, e3)


def _attn_kernel(qT_ref, k_ref, vT_ref, oT_ref, acc_ref, s_ref, *, nqb, chunk):
    assert TQ == 2 * T
    kpos = lax.broadcasted_iota(jnp.int32, (T, TQ), 0)
    qpos = lax.broadcasted_iota(jnp.int32, (T, TQ), 1)

    def diag_mask(r):
        kp = kpos + r * T
        if chunk > 1:
            return (kp // chunk) <= (qpos // chunk)
        return kp <= qpos

    def block_scores(qi, j, slot):
        for g in range(G):
            k = k_ref[g, pl.ds(pl.multiple_of(j * T, T), T), :]
            s_ref[slot, g] = _dot(k, qT_ref[qi, g])

    def q_body(qb, _):
        scores = functools.partial(block_scores, qb)

        def process(j, slot, stats, mask):
            out = []
            for g in range(G):
                m = stats[g]
                s = s_ref[slot, g]
                if mask is not None:
                    s = jnp.where(mask, s, NEG_BIG)
                m_new = jnp.maximum(m, jnp.max(s, axis=0, keepdims=True))
                p = jnp.exp2(s - m_new).astype(BF16)
                pv = _dot(vT_ref[j, g], p)
                acc_ref[g] = jnp.exp2(m - m_new) * acc_ref[g] + pv
                out.append(m_new)
            return tuple(out)

        def pair(jj, stats):
            j = 2 * jj
            scores(j + 1, 1)
            stats = process(j, 0, stats, None)
            scores(j + 2, 0)
            return process(j + 1, 1, stats, None)

        def quad(jj, stats):
            return pair(2 * jj + 1, pair(2 * jj, stats))

        acc_ref[...] = jnp.zeros_like(acc_ref)
        init = tuple(jnp.full((1, TQ), NEG_BIG, F32) for _ in range(G))
        def octet(jj, stats):
            return quad(2 * jj + 1, quad(2 * jj, stats))

        n8, n4 = qb // 4, qb // 2
        stats = lax.fori_loop(0, n8, octet, init)
        stats = lax.fori_loop(2 * n8, n4, quad, stats)
        stats = lax.fori_loop(2 * n4, qb, pair, stats)
        j = 2 * qb
        scores(j + 1, 1)
        stats = process(j, 0, stats, diag_mask(0))
        block_scores(jnp.minimum(qb + 1, nqb - 1), 0, 0)
        stats = process(j + 1, 1, stats, diag_mask(1))
        for g in range(G):
            a = acc_ref[g]
            oT_ref[qb, g * HEAD_V:(g + 1) * HEAD_V, :] = (a[:HEAD_V] / a[HEAD_V:HEAD_V + 1]).astype(BF16)
        return 0

    block_scores(0, 0, 0)
    lax.fori_loop(0, nqb, q_body, 0)


def _attention(qT, k, vT, chunk):
    B, nqb = qT.shape[0], qT.shape[1]
    S = nqb * TQ
    kern = functools.partial(_attn_kernel, nqb=nqb, chunk=chunk)
    return pl.pallas_call(
        kern,
        grid=(B, HEADS // G),
        in_specs=[
            pl.BlockSpec((None, nqb, G, KPAD, TQ), lambda b, h: (b, 0, h, 0, 0)),
            pl.BlockSpec((None, G, S, KPAD), lambda b, h: (b, h, 0, 0)),
            pl.BlockSpec((None, S // T, G, VROWS, T), lambda b, h: (b, 0, h, 0, 0)),
        ],
        out_specs=pl.BlockSpec((None, nqb, G * HEAD_V, TQ), lambda b, h: (b, 0, h, 0)),
        out_shape=jax.ShapeDtypeStruct((B, nqb, HEADS * HEAD_V, TQ), BF16),
        scratch_shapes=[pltpu.VMEM((G, VROWS, TQ), F32), pltpu.VMEM((2, G, T, TQ), F32)],
        compiler_params=_params(("parallel", "parallel")),
        name="attn_chunk%d" % chunk,
    )(qT, k, vT)


def _glu_kernel(x_ref, gmix_ref, wc_ref, u_ref):
    h = _rms(x_ref[...], gmix_ref[...]).astype(BF16)
    ci = _dot(h, wc_ref[...])
    u_ref[...] = ci[:, :CONV_CHANNELS] * jax.nn.sigmoid(ci[:, CONV_CHANNELS:])


def _glu(x, gmix, wc, tm):
    B, S, D = x.shape
    return pl.pallas_call(
        _glu_kernel,
        grid=(B, S // tm),
        in_specs=[pl.BlockSpec((None, tm, D), lambda b, i: (b, i, 0)),
                  _const_spec(gmix.shape), _const_spec(wc.shape)],
        out_specs=pl.BlockSpec((None, tm, CONV_CHANNELS), lambda b, i: (b, i, 0)),
        out_shape=jax.ShapeDtypeStruct((B, S, CONV_CHANNELS), F32),
        compiler_params=_params(("parallel", "parallel")),
        name="glu",
    )(x, gmix, wc)


def _conv_kernel(u_ref, uprev_ref, w_ref, b_ref, lng_ref, lnb_ref, o_ref, buf_ref, *, ts, rows):
    n = HALO + ts
    first = pl.program_id(1) == 0
    buf_ref[0, 0:HALO, :] = jnp.where(first, 0.0, uprev_ref[...])
    buf_ref[0, HALO:n, :] = u_ref[...]
    for r in range(1, 8):
        buf_ref[r, 0:n - 8, :] = buf_ref[0, r:r + n - 8, :]
    off = HALO - (CONV_WIDTH - 1)
    w = w_ref[...]
    for c in range(ts // rows):
        acc = jnp.zeros((rows, CONV_CHANNELS), F32)
        for kk in range(CONV_WIDTH):
            a, r = divmod(off + kk, 8)
            base = c * rows + 8 * a
            acc = acc + buf_ref[r, base:base + rows, :] * w[kk:kk + 1, :]
        y = acc + b_ref[...]
        mu = jnp.mean(y, axis=-1, keepdims=True)
        d = y - mu
        var = jnp.mean(d * d, axis=-1, keepdims=True)
        z = d * lax.rsqrt(var + LN_EPS) * lng_ref[...] + lnb_ref[...]
        o_ref[c * rows:(c + 1) * rows, :] = (z * jax.nn.sigmoid(z)).astype(BF16)


def _conv(u, w, b, lng, lnb, ts):
    B, S, C = u.shape
    rows = 32
    kern = functools.partial(_conv_kernel, ts=ts, rows=rows)
    per = ts // HALO
    return pl.pallas_call(
        kern,
        grid=(B, S // ts),
        in_specs=[
            pl.BlockSpec((None, ts, C), lambda b, i: (b, i, 0)),
            pl.BlockSpec((None, HALO, C), lambda b, i: (b, jnp.maximum(i * per - 1, 0), 0)),
            _const_spec(w.shape), _const_spec(b.shape), _const_spec(lng.shape), _const_spec(lnb.shape),
        ],
        out_specs=pl.BlockSpec((None, ts, C), lambda b, i: (b, i, 0)),
        out_shape=jax.ShapeDtypeStruct((B, S, C), BF16),
        scratch_shapes=[pltpu.VMEM((8, HALO + ts, C), F32)],
        compiler_params=_params(("parallel", "parallel")),
        name="conv",
    )(u, u, w, b, lng, lnb)


def _branch_out_kernel(x_ref, gmix_ref, wg_ref, bg_ref, oaT_ref, obT_ref, uc_ref,
                       wa_ref, wb_ref, wc_ref, wo_ref, o_ref):
    x = x_ref[...]
    D = x.shape[-1]
    h = _rms(x, gmix_ref[...]).astype(BF16)
    g = jax.nn.sigmoid(_dot(h, wg_ref[...]) + bg_ref[...])
    y = g[:, 0:D] * _dot_tn(oaT_ref[...], wa_ref[...])
    y = y + g[:, D:2 * D] * _dot_tn(obT_ref[...], wb_ref[...])
    y = y + g[:, 2 * D:3 * D] * _dot(uc_ref[...], wc_ref[...])
    o_ref[...] = x + _dot(y.astype(BF16), wo_ref[...])


def _branch_out(x, gmix, wg, bg, oaT, obT, uc, wa, wb, wc, wo):
    B, S, D = x.shape
    C = HEADS * HEAD_V
    return pl.pallas_call(
        _branch_out_kernel,
        grid=(B, S // TQ),
        in_specs=[
            pl.BlockSpec((None, TQ, D), lambda b, i: (b, i, 0)),
            _const_spec(gmix.shape), _const_spec(wg.shape), _const_spec(bg.shape),
            pl.BlockSpec((None, None, C, TQ), lambda b, i: (b, i, 0, 0)),
            pl.BlockSpec((None, None, C, TQ), lambda b, i: (b, i, 0, 0)),
            pl.BlockSpec((None, TQ, CONV_CHANNELS), lambda b, i: (b, i, 0)),
            _const_spec(wa.shape), _const_spec(wb.shape), _const_spec(wc.shape), _const_spec(wo.shape),
        ],
        out_specs=pl.BlockSpec((None, TQ, D), lambda b, i: (b, i, 0)),
        out_shape=jax.ShapeDtypeStruct((B, S, D), F32),
        compiler_params=_params(("parallel", "parallel")),
        name="branch_out",
    )(x, gmix, wg, bg, oaT, obT, uc, wa, wb, wc, wo)


def _ffn_kernel(x_ref, g_ref, wg_ref, wu_ref, wd_ref, o_ref):
    x = x_ref[...]
    h = _rms(x, g_ref[...]).astype(BF16)
    a = _dot(h, wg_ref[...])
    u = _dot(h, wu_ref[...])
    ff = (a * jax.nn.sigmoid(a) * u).astype(BF16)
    o_ref[...] = x + _dot(ff, wd_ref[...])


def _ffn(x, g, wg, wu, wd, tm):
    B, S, D = x.shape
    return pl.pallas_call(
        _ffn_kernel,
        grid=(B, S // tm),
        in_specs=[pl.BlockSpec((None, tm, D), lambda b, i: (b, i, 0)),
                  _const_spec(g.shape), _const_spec(wg.shape), _const_spec(wu.shape), _const_spec(wd.shape)],
        out_specs=pl.BlockSpec((None, tm, D), lambda b, i: (b, i, 0)),
        out_shape=jax.ShapeDtypeStruct((B, S, D), F32),
        compiler_params=_params(("parallel", "parallel")),
        name="ffn",
    )(x, g, wg, wu, wd)


def _final_norm_kernel(x_ref, g_ref, o_ref):
    o_ref[...] = _rms(x_ref[...], g_ref[...])


def _final_norm(x, g, tm):
    B, S, D = x.shape
    return pl.pallas_call(
        _final_norm_kernel,
        grid=(B, S // tm),
        in_specs=[pl.BlockSpec((None, tm, D), lambda b, i: (b, i, 0)), _const_spec(g.shape)],
        out_specs=pl.BlockSpec((None, tm, D), lambda b, i: (b, i, 0)),
        out_shape=jax.ShapeDtypeStruct((B, S, D), F32),
        compiler_params=_params(("parallel", "parallel")),
        name="final_norm",
    )(x, g)


def _prep_weights(w_in, w_uq, w_ukv, b_forget):
    L, D, _ = w_in.shape
    w_in, w_uq, w_ukv = w_in.astype(BF16), w_uq.astype(BF16), w_ukv.astype(BF16)
    sizes = (Q_LORA, KV_LORA, MLA_ROPE, FOX_HEADS * FOX_HEAD_DIM, FOX_HEADS * FOX_HEAD_DIM,
             FOX_HEADS * FOX_HEAD_DIM, FOX_HEADS, 2 * CONV_CHANNELS, N_BRANCHES * D)
    offs = [0]
    for sz in sizes:
        offs.append(offs[-1] + sz)
    w_cq, w_ckv, w_kr, w_qb, w_kb, w_vb, w_f, w_conv, w_gate = [
        w_in[:, :, offs[i]:offs[i + 1]] for i in range(len(sizes))]
    half = MLA_ROPE // 2
    z = lambda n: jnp.zeros((L, D, n), w_in.dtype)
    kr_a = jnp.concatenate([z(MLA_NOPE), w_kr, z(KPAD - MLA_NOPE - MLA_ROPE)], axis=2)
    kr_b = jnp.concatenate([z(MLA_NOPE), w_kr[:, :, half:], w_kr[:, :, :half],
                            z(KPAD - MLA_NOPE - MLA_ROPE)], axis=2)
    wa = jnp.concatenate([w_cq, w_ckv, kr_a, kr_b], axis=2).astype(BF16)

    dq = MLA_NOPE + MLA_ROPE
    wq = w_uq.reshape(L, Q_LORA, MLA_HEADS, dq)
    wq = jnp.pad(wq, ((0, 0), (0, 0), (0, 0), (0, KPAD - dq)))
    wqT = wq.transpose(0, 2, 3, 1).reshape(L, MLA_HEADS * KPAD, Q_LORA).astype(BF16)
    wkv = w_ukv.reshape(L, KV_LORA, MLA_HEADS, MLA_NOPE + MLA_V)
    wkn = jnp.pad(wkv[..., :MLA_NOPE], ((0, 0), (0, 0), (0, 0), (0, KPAD - MLA_NOPE)))
    wkn = wkn.reshape(L, KV_LORA, MLA_HEADS * KPAD).astype(BF16)
    wvT = wkv[..., MLA_NOPE:].transpose(0, 2, 3, 1).reshape(L, MLA_HEADS * MLA_V, KV_LORA).astype(BF16)

    wqbT = w_qb.transpose(0, 2, 1).astype(BF16)
    wvbT = w_vb.transpose(0, 2, 1).astype(BF16)
    wkb = w_kb.reshape(L, D, FOX_HEADS, FOX_HEAD_DIM)
    wkb = jnp.pad(wkb, ((0, 0), (0, 0), (0, 0), (0, KPAD - FOX_HEAD_DIM)))
    wkb = wkb.reshape(L, D, FOX_HEADS * KPAD).astype(BF16)
    wf = jnp.pad(w_f, ((0, 0), (0, 0), (0, KPAD - FOX_HEADS))).astype(BF16)
    bf = jnp.pad(b_forget.astype(F32), ((0, 0), (0, KPAD - FOX_HEADS))).reshape(L, 1, KPAD)
    return dict(wa=wa, wqT=wqT, wkn=wkn, wvT=wvT, wqbT=wqbT, wkb=wkb, wvbT=wvbT, wf=wf, bf=bf,
                wconv=w_conv.astype(BF16), wgate=w_gate.astype(BF16))


def kernel(x, positions, norm_mix_g, w_in, b_gate, q_norm_g, w_uq, kv_norm_g, w_ukv, b_forget,
           dw_kernel, dw_bias, conv_ln_g, conv_ln_b, w_bo_a, w_bo_b, w_bo_c, w_out, norm_ffn_g,
           w_ffn_gate, w_ffn_up, w_ffn_down, final_norm_g):
    B, S, D = x.shape
    L = w_in.shape[0]
    assert S % TQ == 0 and D == 1024
    tm = TQ
    tf = T
    row = lambda a: a.reshape(L, 1, -1).astype(F32)
    pw = _prep_weights(w_in, w_uq, w_ukv, b_forget)
    gmix, bg, qg, kvg = row(norm_mix_g), row(b_gate), row(q_norm_g), row(kv_norm_g)
    dwb, lng, lnb, gffn = row(dw_bias), row(conv_ln_g), row(conv_ln_b), row(norm_ffn_g)
    woa, wob, woc, wo = (w.astype(BF16) for w in (w_bo_a, w_bo_b, w_bo_c, w_out))
    wfg, wfu, wfd = (w.astype(BF16) for w in (w_ffn_gate, w_ffn_up, w_ffn_down))

    cosT, sinT, cos128, sin128 = _rope_tables(positions)
    for l in range(L):
        qaT, ka, vaT = _mla_in(x, gmix[l], pw["wa"][l], qg[l], kvg[l], pw["wqT"][l], pw["wkn"][l],
                               pw["wvT"][l], cosT, sinT, cos128, sin128, tm)
        oaT = _attention(qaT, ka, vaT, CHUNK)
        qbT, kb, vbT = _fox_in(x, gmix[l], pw["wqbT"][l], pw["wkb"][l], pw["wvbT"][l],
                               pw["wf"][l], pw["bf"][l], tm)
        obT = _attention(qbT, kb, vbT, 1)
        u = _glu(x, gmix[l], pw["wconv"][l], tm)
        uc = _conv(u, dw_kernel[l].astype(F32), dwb[l], lng[l], lnb[l], tm)
        x = _branch_out(x, gmix[l], pw["wgate"][l], bg[l], oaT, obT, uc, woa[l], wob[l], woc[l], wo[l])
        x = _ffn(x, gffn[l], wfg[l], wfu[l], wfd[l], tf)
    return _final_norm(x, final_norm_g.reshape(1, D).astype(F32), tm)
```

```python
import functools
import math

import jax
import jax.numpy as jnp
from jax import lax
from jax.experimental import pallas as pl
from jax.experimental.pallas import tpu as pltpu

F32 = jnp.float32
BF16 = jnp.bfloat16

MLA_HEADS = 8
MLA_NOPE = 64
MLA_ROPE = 32
MLA_V = 64
Q_LORA = 384
KV_LORA = 256
ROPE_THETA = 10000.0
FOX_HEADS = 8
FOX_HEAD_DIM = 64
CONV_CHANNELS = 512
CONV_WIDTH = 31
CHUNK = 64
N_BRANCHES = 3
RMS_EPS = 1e-6
LN_EPS = 1e-5

HEADS = 8
HEAD_V = 64
VROWS = 80
KPAD = 128
FROWS = 16
T = 256
TQ = 512
G = 2
HALO = 32
LOG2E = math.log2(math.e)
NEG_BIG = -1e30
VMEM_LIMIT = 56 * 1024 * 1024


def _rms(x, g):
    ms = jnp.mean(x * x, axis=-1, keepdims=True)
    return x * lax.rsqrt(ms + RMS_EPS) * g


def _split3(v):
    hi = v.astype(BF16)
    r = v - hi.astype(F32)
    mid = r.astype(BF16)
    lo = (r - mid.astype(F32)).astype(BF16)
    return hi, mid, lo


def _dot(a, b):
    return jnp.dot(a, b, preferred_element_type=F32)


def _dot_nt(a, b):
    return lax.dot_general(a, b, (((1,), (1,)), ((), ())), preferred_element_type=F32)


def _dot_tn(a, b):
    return lax.dot_general(a, b, (((0,), (0,)), ((), ())), preferred_element_type=F32)


def _store_values(vT_ref, vT, tm):
    vT = vT.astype(BF16).reshape(HEADS, HEAD_V, tm)
    row = lax.broadcasted_iota(jnp.int32, (HEADS, VROWS - HEAD_V, T), 1)
    ones_row = jnp.where(row < 1, 1.0, 0.0).astype(BF16)
    for t in range(tm // T):
        vT_ref[t, :, 0:HEAD_V, :] = vT[:, :, t * T:(t + 1) * T]
        vT_ref[t, :, HEAD_V:, :] = ones_row


def _params(sem):
    return pltpu.CompilerParams(dimension_semantics=sem, vmem_limit_bytes=VMEM_LIMIT)


def _const_spec(shape):
    n = len(shape)
    return pl.BlockSpec(shape, lambda *_: (0,) * n)


def _rope_kernel(prow_ref, pcol_ref, invc_ref, invl_ref, sgn_ref, cosT_ref, sinT_ref, cos_ref, sin_ref):
    angT = prow_ref[...].astype(F32) * invc_ref[...]
    cosT_ref[...] = jnp.cos(angT)
    sinT_ref[...] = jnp.sin(angT)
    ang = pcol_ref[...].astype(F32) * invl_ref[...]
    sgn = sgn_ref[...]
    cos_ref[...] = jnp.cos(ang) * jnp.abs(sgn)
    sin_ref[...] = jnp.sin(ang) * sgn


def _rope_tables(positions):
    B, S = positions.shape
    half = MLA_ROPE // 2
    inv_freq = 1.0 / (ROPE_THETA ** (jnp.arange(0, MLA_ROPE, 2, dtype=F32) / MLA_ROPE))
    inv_col = inv_freq.reshape(half, 1)
    zeros = jnp.zeros((MLA_NOPE,), F32)
    tail = jnp.zeros((KPAD - MLA_NOPE - MLA_ROPE,), F32)
    inv_lane = jnp.concatenate([zeros, inv_freq, inv_freq, tail]).reshape(1, KPAD)
    ones = jnp.ones((half,), F32)
    sgn_lane = jnp.concatenate([zeros, -ones, ones, tail]).reshape(1, KPAD)
    TS = min(S, 1024)
    return pl.pallas_call(
        _rope_kernel,
        grid=(B, S // TS),
        in_specs=[
            pl.BlockSpec((None, 1, TS), lambda b, i: (b, 0, i)),
            pl.BlockSpec((None, TS, 1), lambda b, i: (b, i, 0)),
            _const_spec((half, 1)),
            _const_spec((1, KPAD)),
            _const_spec((1, KPAD)),
        ],
        out_specs=[
            pl.BlockSpec((None, half, TS), lambda b, i: (b, 0, i)),
            pl.BlockSpec((None, half, TS), lambda b, i: (b, 0, i)),
            pl.BlockSpec((None, TS, KPAD), lambda b, i: (b, i, 0)),
            pl.BlockSpec((None, TS, KPAD), lambda b, i: (b, i, 0)),
        ],
        out_shape=[
            jax.ShapeDtypeStruct((B, half, S), F32),
            jax.ShapeDtypeStruct((B, half, S), F32),
            jax.ShapeDtypeStruct((B, S, KPAD), F32),
            jax.ShapeDtypeStruct((B, S, KPAD), F32),
        ],
        compiler_params=_params(("parallel", "parallel")),
        name="rope_tables",
    )(positions.reshape(B, 1, S), positions.reshape(B, S, 1), inv_col, inv_lane, sgn_lane)


def _mla_in_kernel(x_ref, gmix_ref, wa_ref, qg_ref, kvg_ref, wqT_ref, wkn_ref, wvT_ref,
                   cosT_ref, sinT_ref, cos_ref, sin_ref, qT_ref, k_ref, vT_ref, *, tm):
    h = _rms(x_ref[...], gmix_ref[...]).astype(BF16)
    pa = _dot(h, wa_ref[...])
    cqn = _rms(pa[:, 0:Q_LORA], qg_ref[...]).astype(BF16)
    ckvn = _rms(pa[:, Q_LORA:Q_LORA + KV_LORA], kvg_ref[...]).astype(BF16)
    o = Q_LORA + KV_LORA
    kra = pa[:, o:o + KPAD]
    krb = pa[:, o + KPAD:o + 2 * KPAD]

    qT = _dot_nt(wqT_ref[...], cqn) * (LOG2E * (MLA_NOPE + MLA_ROPE) ** -0.5)
    qT = qT.reshape(HEADS, KPAD, tm)
    half = MLA_ROPE // 2
    x1 = qT[:, MLA_NOPE:MLA_NOPE + half, :]
    x2 = qT[:, MLA_NOPE + half:MLA_NOPE + 2 * half, :]
    c = cosT_ref[...][None]
    s = sinT_ref[...][None]
    qT = jnp.concatenate(
        [qT[:, :MLA_NOPE, :], x1 * c - x2 * s, x2 * c + x1 * s, qT[:, MLA_NOPE + 2 * half:, :]],
        axis=1).astype(BF16)
    for t in range(tm // TQ):
        qT_ref[t] = qT[:, :, t * TQ:(t + 1) * TQ]
    _store_values(vT_ref, _dot_nt(wvT_ref[...], ckvn), tm)

    krope = kra * cos_ref[...] + krb * sin_ref[...]
    kn = _dot(ckvn, wkn_ref[...])
    for hd in range(HEADS):
        k_ref[hd] = (kn[:, hd * KPAD:(hd + 1) * KPAD] + krope).astype(BF16)


def _mla_in(x, gmix, wa, qg, kvg, wqT, wkn, wvT, cosT, sinT, cos128, sin128, tm):
    B, S, D = x.shape
    nt = tm // T
    kern = functools.partial(_mla_in_kernel, tm=tm)
    half = MLA_ROPE // 2
    return pl.pallas_call(
        kern,
        grid=(B, S // tm),
        in_specs=[
            pl.BlockSpec((None, tm, D), lambda b, i: (b, i, 0)),
            _const_spec(gmix.shape), _const_spec(wa.shape), _const_spec(qg.shape),
            _const_spec(kvg.shape), _const_spec(wqT.shape), _const_spec(wkn.shape),
            _const_spec(wvT.shape),
            pl.BlockSpec((None, half, tm), lambda b, i: (b, 0, i)),
            pl.BlockSpec((None, half, tm), lambda b, i: (b, 0, i)),
            pl.BlockSpec((None, tm, KPAD), lambda b, i: (b, i, 0)),
            pl.BlockSpec((None, tm, KPAD), lambda b, i: (b, i, 0)),
        ],
        out_specs=[
            pl.BlockSpec((None, tm // TQ, HEADS, KPAD, TQ), lambda b, i: (b, i, 0, 0, 0)),
            pl.BlockSpec((None, HEADS, tm, KPAD), lambda b, i: (b, 0, i, 0)),
            pl.BlockSpec((None, nt, HEADS, VROWS, T), lambda b, i: (b, i, 0, 0, 0)),
        ],
        out_shape=[
            jax.ShapeDtypeStruct((B, S // TQ, HEADS, KPAD, TQ), BF16),
            jax.ShapeDtypeStruct((B, HEADS, S, KPAD), BF16),
            jax.ShapeDtypeStruct((B, S // T, HEADS, VROWS, T), BF16),
        ],
        compiler_params=_params(("parallel", "parallel")),
        name="mla_in",
    )(x, gmix, wa, qg, kvg, wqT, wkn, wvT, cosT, sinT, cos128, sin128)


def _fox_in_kernel(x_ref, gmix_ref, wqT_ref, wkp_ref, wvT_ref, wf_ref, bf_ref, tri_ref, e3_ref,
                   qT_ref, k_ref, vT_ref, carry_ref, *, tm):
    @pl.when(pl.program_id(1) == 0)
    def _():
        carry_ref[...] = jnp.zeros_like(carry_ref)

    h = _rms(x_ref[...], gmix_ref[...]).astype(BF16)
    qT = (_dot_nt(wqT_ref[...], h) * (LOG2E * FOX_HEAD_DIM ** -0.5)).astype(BF16)
    qT = qT.reshape(HEADS, FOX_HEAD_DIM, tm)
    row = lax.broadcasted_iota(jnp.int32, (HEADS, 16, TQ), 1)
    ones_rows = jnp.where(row < 3, 1.0, 0.0).astype(BF16)
    zero_rows = jnp.zeros((HEADS, KPAD - FOX_HEAD_DIM - 16, TQ), BF16)
    for t in range(tm // TQ):
        qT_ref[t, :, 0:FOX_HEAD_DIM, :] = qT[:, :, t * TQ:(t + 1) * TQ]
        qT_ref[t, :, FOX_HEAD_DIM:FOX_HEAD_DIM + 16, :] = ones_rows
        qT_ref[t, :, FOX_HEAD_DIM + 16:, :] = zero_rows
    _store_values(vT_ref, _dot_nt(wvT_ref[...], h), tm)

    f = _dot_nt(wf_ref[...], h) + bf_ref[...]
    frow = lax.broadcasted_iota(jnp.int32, f.shape, 0)
    logf = jnp.where(frow < FOX_HEADS, jax.nn.log_sigmoid(f), 0.0)
    c3 = _dot(jnp.concatenate(_split3(logf), axis=0), tri_ref[...])
    cum = c3[0:FROWS] + c3[FROWS:2 * FROWS] + c3[2 * FROWS:] + carry_ref[...]
    carry_ref[...] = jnp.broadcast_to(cum[:, tm - 1:tm], cum.shape)
    pieces = jnp.concatenate(
        _split3(cum * (-LOG2E)) + (jnp.zeros((KPAD - 3 * FROWS, tm), BF16),), axis=0)
    place = _dot_tn(pieces, e3_ref[...])
    kp = _dot(h, wkp_ref[...]) + place
    for hd in range(HEADS):
        k_ref[hd] = kp[:, hd * KPAD:(hd + 1) * KPAD].astype(BF16)


def _fox_in(x, gmix, wqT, wkp, wvT, wf, bf, tm):
    B, S, D = x.shape
    nt = tm // T
    r = lax.broadcasted_iota(jnp.int32, (tm, tm), 0)
    c = lax.broadcasted_iota(jnp.int32, (tm, tm), 1)
    tri = (r <= c).astype(BF16)
    bf = jnp.broadcast_to(bf, (FROWS, tm))
    rr = jnp.arange(KPAD)
    cc = jnp.arange(HEADS * KPAD)
    part, hd = rr // FROWS, rr % FROWS
    e3 = ((cc[None, :] == (hd * KPAD + FOX_HEAD_DIM + part)[:, None])
          & ((hd < HEADS) & (part < 3))[:, None]).astype(BF16)
    kern = functools.partial(_fox_in_kernel, tm=tm)
    return pl.pallas_call(
        kern,
        grid=(B, S // tm),
        in_specs=[
            pl.BlockSpec((None, tm, D), lambda b, i: (b, i, 0)),
            _const_spec(gmix.shape), _const_spec(wqT.shape), _const_spec(wkp.shape),
            _const_spec(wvT.shape), _const_spec(wf.shape), _const_spec(bf.shape),
            _const_spec(tri.shape), _const_spec(e3.shape),
        ],
        out_specs=[
            pl.BlockSpec((None, tm // TQ, HEADS, KPAD, TQ), lambda b, i: (b, i, 0, 0, 0)),
            pl.BlockSpec((None, HEADS, tm, KPAD), lambda b, i: (b, 0, i, 0)),
            pl.BlockSpec((None, nt, HEADS, VROWS, T), lambda b, i: (b, i, 0, 0, 0)),
        ],
        out_shape=[
            jax.ShapeDtypeStruct((B, S // TQ, HEADS, KPAD, TQ), BF16),
            jax.ShapeDtypeStruct((B, HEADS, S, KPAD), BF16),
            jax.ShapeDtypeStruct((B, S // T, HEADS, VROWS, T), BF16),
        ],
        scratch_shapes=[pltpu.VMEM((FROWS, tm), F32)],
        compiler_params=_params(("parallel", "arbitrary")),
        name="fox_in",
    )(x, gmix, wqT, wkp, wvT, wf, bf, tri, e3)


def _attn_kernel(qT_ref, k_ref, vT_ref, oT_ref, acc_ref, s_ref, *, nqb, chunk):
    assert TQ == 2 * T
    kpos = lax.broadcasted_iota(jnp.int32, (T, T), 0)
    qpos = lax.broadcasted_iota(jnp.int32, (T, T), 1)
    tile_mask = ((kpos // chunk) <= (qpos // chunk)) if chunk > 1 else (kpos <= qpos)

    def block_scores(qi, j, slot):
        for g in range(G):
            k = k_ref[g, pl.ds(pl.multiple_of(j * T, T), T), :]
            s_ref[slot, g] = _dot(k, qT_ref[qi, g])

    def q_block(qb, sa):
        sb = sa + 1
        scores = functools.partial(block_scores, qb)

        def process(j, slot, stats):
            out = []
            for g in range(G):
                m = stats[g]
                s = s_ref[slot, g]
                m_new = jnp.maximum(m, jnp.max(s, axis=0, keepdims=True))
                p = jnp.exp2(s - m_new).astype(BF16)
                pv = _dot(vT_ref[j, g], p)
                acc_ref[g] = jnp.exp2(m - m_new) * acc_ref[g] + pv
                out.append(m_new)
            return tuple(out)

        def last_scores(j):
            for g in range(G):
                k = k_ref[g, pl.ds(pl.multiple_of(j * T, T), T), :]
                s_ref[sb, g, :, T:] = _dot(k, qT_ref[qb, g, :, T:])

        def diagonal(j, stats):
            for g in range(G):
                m = stats[g]
                s0 = s_ref[sa, g]
                s0 = jnp.concatenate([jnp.where(tile_mask, s0[:, :T], NEG_BIG), s0[:, T:]], axis=1)
                s1 = jnp.where(tile_mask, s_ref[sb, g, :, T:], NEG_BIG)
                mx = jnp.max(s0, axis=0, keepdims=True)
                mx = jnp.concatenate(
                    [mx[:, :T], jnp.maximum(mx[:, T:], jnp.max(s1, axis=0, keepdims=True))], axis=1)
                m_new = jnp.maximum(m, mx)
                p0 = jnp.exp2(s0 - m_new).astype(BF16)
                p1 = jnp.exp2(s1 - m_new[:, T:]).astype(BF16)
                alpha = jnp.exp2(m - m_new)
                acc_ref[g] = alpha * acc_ref[g] + _dot(vT_ref[j, g], p0)
                acc_ref[g, :, T:] = acc_ref[g, :, T:] + _dot(vT_ref[j + 1, g], p1)

        def pair(jj, stats):
            j = 2 * jj
            scores(j + 1, sb)
            stats = process(j, sa, stats)
            scores(j + 2, sa)
            return process(j + 1, sb, stats)

        def quad(jj, stats):
            return pair(2 * jj + 1, pair(2 * jj, stats))

        acc_ref[...] = jnp.zeros_like(acc_ref)
        init = tuple(jnp.full((1, TQ), NEG_BIG, F32) for _ in range(G))

        def octet(jj, stats):
            return quad(2 * jj + 1, quad(2 * jj, stats))

        n8, n4 = qb // 4, qb // 2
        stats = lax.fori_loop(0, n8, octet, init)
        stats = lax.fori_loop(2 * n8, n4, quad, stats)
        stats = lax.fori_loop(2 * n4, qb, pair, stats)
        j = 2 * qb
        last_scores(j + 1)
        block_scores(jnp.minimum(qb + 1, nqb - 1), 0, 2 - sa)
        diagonal(j, stats)
        for g in range(G):
            a = acc_ref[g]
            oT_ref[qb, g * HEAD_V:(g + 1) * HEAD_V, :] = (a[:HEAD_V] / a[HEAD_V:HEAD_V + 1]).astype(BF16)

    def q_pair(i, _):
        q_block(2 * i, 0)
        q_block(2 * i + 1, 2)
        return 0

    assert nqb % 2 == 0
    block_scores(0, 0, 0)
    lax.fori_loop(0, nqb // 2, q_pair, 0)


def _attention(qT, k, vT, chunk):
    B, nqb = qT.shape[0], qT.shape[1]
    S = nqb * TQ
    kern = functools.partial(_attn_kernel, nqb=nqb, chunk=chunk)
    return pl.pallas_call(
        kern,
        grid=(B, HEADS // G),
        in_specs=[
            pl.BlockSpec((None, nqb, G, KPAD, TQ), lambda b, h: (b, 0, h, 0, 0)),
            pl.BlockSpec((None, G, S, KPAD), lambda b, h: (b, h, 0, 0)),
            pl.BlockSpec((None, S // T, G, VROWS, T), lambda b, h: (b, 0, h, 0, 0)),
        ],
        out_specs=pl.BlockSpec((None, nqb, G * HEAD_V, TQ), lambda b, h: (b, 0, h, 0)),
        out_shape=jax.ShapeDtypeStruct((B, nqb, HEADS * HEAD_V, TQ), BF16),
        scratch_shapes=[pltpu.VMEM((G, VROWS, TQ), F32), pltpu.VMEM((4, G, T, TQ), F32)],
        compiler_params=_params(("parallel", "parallel")),
        name="attn_chunk%d" % chunk,
    )(qT, k, vT)


def _glu_kernel(x_ref, gmix_ref, wc_ref, u_ref):
    h = _rms(x_ref[...], gmix_ref[...]).astype(BF16)
    ci = _dot(h, wc_ref[...])
    u_ref[...] = ci[:, :CONV_CHANNELS] * jax.nn.sigmoid(ci[:, CONV_CHANNELS:])


def _glu(x, gmix, wc, tm):
    B, S, D = x.shape
    return pl.pallas_call(
        _glu_kernel,
        grid=(B, S // tm),
        in_specs=[pl.BlockSpec((None, tm, D), lambda b, i: (b, i, 0)),
                  _const_spec(gmix.shape), _const_spec(wc.shape)],
        out_specs=pl.BlockSpec((None, tm, CONV_CHANNELS), lambda b, i: (b, i, 0)),
        out_shape=jax.ShapeDtypeStruct((B, S, CONV_CHANNELS), F32),
        compiler_params=_params(("parallel", "parallel")),
        name="glu",
    )(x, gmix, wc)


def _conv_kernel(u_ref, uprev_ref, w_ref, b_ref, lng_ref, lnb_ref, o_ref, buf_ref, *, ts, rows):
    n = HALO + ts
    first = pl.program_id(1) == 0
    buf_ref[0, 0:HALO, :] = jnp.where(first, 0.0, uprev_ref[...])
    buf_ref[0, HALO:n, :] = u_ref[...]
    for r in range(1, 8):
        buf_ref[r, 0:n - 8, :] = buf_ref[0, r:r + n - 8, :]
    off = HALO - (CONV_WIDTH - 1)
    w = w_ref[...]
    for c in range(ts // rows):
        acc = jnp.zeros((rows, CONV_CHANNELS), F32)
        for kk in range(CONV_WIDTH):
            a, r = divmod(off + kk, 8)
            base = c * rows + 8 * a
            acc = acc + buf_ref[r, base:base + rows, :] * w[kk:kk + 1, :]
        y = acc + b_ref[...]
        mu = jnp.mean(y, axis=-1, keepdims=True)
        d = y - mu
        var = jnp.mean(d * d, axis=-1, keepdims=True)
        z = d * lax.rsqrt(var + LN_EPS) * lng_ref[...] + lnb_ref[...]
        o_ref[c * rows:(c + 1) * rows, :] = (z * jax.nn.sigmoid(z)).astype(BF16)


def _conv(u, w, b, lng, lnb, ts):
    B, S, C = u.shape
    rows = 32
    kern = functools.partial(_conv_kernel, ts=ts, rows=rows)
    per = ts // HALO
    return pl.pallas_call(
        kern,
        grid=(B, S // ts),
        in_specs=[
            pl.BlockSpec((None, ts, C), lambda b, i: (b, i, 0)),
            pl.BlockSpec((None, HALO, C), lambda b, i: (b, jnp.maximum(i * per - 1, 0), 0)),
            _const_spec(w.shape), _const_spec(b.shape), _const_spec(lng.shape), _const_spec(lnb.shape),
        ],
        out_specs=pl.BlockSpec((None, ts, C), lambda b, i: (b, i, 0)),
        out_shape=jax.ShapeDtypeStruct((B, S, C), BF16),
        scratch_shapes=[pltpu.VMEM((8, HALO + ts, C), F32)],
        compiler_params=_params(("parallel", "parallel")),
        name="conv",
    )(u, u, w, b, lng, lnb)


def _branch_out_kernel(x_ref, gmix_ref, wg_ref, bg_ref, oaT_ref, obT_ref, uc_ref,
                       wa_ref, wb_ref, wc_ref, wo_ref, o_ref):
    x = x_ref[...]
    D = x.shape[-1]
    h = _rms(x, gmix_ref[...]).astype(BF16)
    g = jax.nn.sigmoid(_dot(h, wg_ref[...]) + bg_ref[...])
    y = g[:, 0:D] * _dot_tn(oaT_ref[...], wa_ref[...])
    y = y + g[:, D:2 * D] * _dot_tn(obT_ref[...], wb_ref[...])
    y = y + g[:, 2 * D:3 * D] * _dot(uc_ref[...], wc_ref[...])
    o_ref[...] = x + _dot(y.astype(BF16), wo_ref[...])


def _branch_out(x, gmix, wg, bg, oaT, obT, uc, wa, wb, wc, wo):
    B, S, D = x.shape
    C = HEADS * HEAD_V
    return pl.pallas_call(
        _branch_out_kernel,
        grid=(B, S // TQ),
        in_specs=[
            pl.BlockSpec((None, TQ, D), lambda b, i: (b, i, 0)),
            _const_spec(gmix.shape), _const_spec(wg.shape), _const_spec(bg.shape),
            pl.BlockSpec((None, None, C, TQ), lambda b, i: (b, i, 0, 0)),
            pl.BlockSpec((None, None, C, TQ), lambda b, i: (b, i, 0, 0)),
            pl.BlockSpec((None, TQ, CONV_CHANNELS), lambda b, i: (b, i, 0)),
            _const_spec(wa.shape), _const_spec(wb.shape), _const_spec(wc.shape), _const_spec(wo.shape),
        ],
        out_specs=pl.BlockSpec((None, TQ, D), lambda b, i: (b, i, 0)),
        out_shape=jax.ShapeDtypeStruct((B, S, D), F32),
        compiler_params=_params(("parallel", "parallel")),
        name="branch_out",
    )(x, gmix, wg, bg, oaT, obT, uc, wa, wb, wc, wo)


def _ffn_kernel(x_ref, g_ref, wg_ref, wu_ref, wd_ref, gout_ref, o_ref, *, final):
    x = x_ref[...]
    h = _rms(x, g_ref[...]).astype(BF16)
    a = _dot(h, wg_ref[...])
    u = _dot(h, wu_ref[...])
    ff = (a * jax.nn.sigmoid(a) * u).astype(BF16)
    y = x + _dot(ff, wd_ref[...])
    o_ref[...] = _rms(y, gout_ref[...]) if final else y


def _ffn(x, g, wg, wu, wd, gout, tm, final):
    B, S, D = x.shape
    return pl.pallas_call(
        functools.partial(_ffn_kernel, final=final),
        grid=(B, S // tm),
        in_specs=[pl.BlockSpec((None, tm, D), lambda b, i: (b, i, 0)),
                  _const_spec(g.shape), _const_spec(wg.shape), _const_spec(wu.shape), _const_spec(wd.shape),
                  _const_spec(gout.shape)],
        out_specs=pl.BlockSpec((None, tm, D), lambda b, i: (b, i, 0)),
        out_shape=jax.ShapeDtypeStruct((B, S, D), F32),
        compiler_params=_params(("parallel", "parallel")),
        name="ffn_final" if final else "ffn",
    )(x, g, wg, wu, wd, gout)


def _prep_weights(w_in, w_uq, w_ukv, b_forget):
    L, D, _ = w_in.shape
    w_in, w_uq, w_ukv = w_in.astype(BF16), w_uq.astype(BF16), w_ukv.astype(BF16)
    sizes = (Q_LORA, KV_LORA, MLA_ROPE, FOX_HEADS * FOX_HEAD_DIM, FOX_HEADS * FOX_HEAD_DIM,
             FOX_HEADS * FOX_HEAD_DIM, FOX_HEADS, 2 * CONV_CHANNELS, N_BRANCHES * D)
    offs = [0]
    for sz in sizes:
        offs.append(offs[-1] + sz)
    w_cq, w_ckv, w_kr, w_qb, w_kb, w_vb, w_f, w_conv, w_gate = [
        w_in[:, :, offs[i]:offs[i + 1]] for i in range(len(sizes))]
    half = MLA_ROPE // 2
    z = lambda n: jnp.zeros((L, D, n), w_in.dtype)
    kr_a = jnp.concatenate([z(MLA_NOPE), w_kr, z(KPAD - MLA_NOPE - MLA_ROPE)], axis=2)
    kr_b = jnp.concatenate([z(MLA_NOPE), w_kr[:, :, half:], w_kr[:, :, :half],
                            z(KPAD - MLA_NOPE - MLA_ROPE)], axis=2)
    wa = jnp.concatenate([w_cq, w_ckv, kr_a, kr_b], axis=2).astype(BF16)

    dq = MLA_NOPE + MLA_ROPE
    wq = w_uq.reshape(L, Q_LORA, MLA_HEADS, dq)
    wq = jnp.pad(wq, ((0, 0), (0, 0), (0, 0), (0, KPAD - dq)))
    wqT = wq.transpose(0, 2, 3, 1).reshape(L, MLA_HEADS * KPAD, Q_LORA).astype(BF16)
    wkv = w_ukv.reshape(L, KV_LORA, MLA_HEADS, MLA_NOPE + MLA_V)
    wkn = jnp.pad(wkv[..., :MLA_NOPE], ((0, 0), (0, 0), (0, 0), (0, KPAD - MLA_NOPE)))
    wkn = wkn.reshape(L, KV_LORA, MLA_HEADS * KPAD).astype(BF16)
    wvT = wkv[..., MLA_NOPE:].transpose(0, 2, 3, 1).reshape(L, MLA_HEADS * MLA_V, KV_LORA).astype(BF16)

    wqbT = w_qb.transpose(0, 2, 1).astype(BF16)
    wvbT = w_vb.transpose(0, 2, 1).astype(BF16)
    wkb = w_kb.reshape(L, D, FOX_HEADS, FOX_HEAD_DIM)
    wkb = jnp.pad(wkb, ((0, 0), (0, 0), (0, 0), (0, KPAD - FOX_HEAD_DIM)))
    wkb = wkb.reshape(L, D, FOX_HEADS * KPAD).astype(BF16)
    wf = jnp.pad(w_f.transpose(0, 2, 1), ((0, 0), (0, FROWS - FOX_HEADS), (0, 0))).astype(BF16)
    bf = jnp.pad(b_forget.astype(F32), ((0, 0), (0, FROWS - FOX_HEADS))).reshape(L, FROWS, 1)
    return dict(wa=wa, wqT=wqT, wkn=wkn, wvT=wvT, wqbT=wqbT, wkb=wkb, wvbT=wvbT, wf=wf, bf=bf,
                wconv=w_conv.astype(BF16), wgate=w_gate.astype(BF16))


def kernel(x, positions, norm_mix_g, w_in, b_gate, q_norm_g, w_uq, kv_norm_g, w_ukv, b_forget,
           dw_kernel, dw_bias, conv_ln_g, conv_ln_b, w_bo_a, w_bo_b, w_bo_c, w_out, norm_ffn_g,
           w_ffn_gate, w_ffn_up, w_ffn_down, final_norm_g):
    B, S, D = x.shape
    L = w_in.shape[0]
    assert S % TQ == 0 and D == 1024
    tm = TQ
    tf = T
    row = lambda a: a.reshape(L, 1, -1).astype(F32)
    pw = _prep_weights(w_in, w_uq, w_ukv, b_forget)
    gmix, bg, qg, kvg = row(norm_mix_g), row(b_gate), row(q_norm_g), row(kv_norm_g)
    dwb, lng, lnb, gffn = row(dw_bias), row(conv_ln_g), row(conv_ln_b), row(norm_ffn_g)
    woa, wob, woc, wo = (w.astype(BF16) for w in (w_bo_a, w_bo_b, w_bo_c, w_out))
    wfg, wfu, wfd = (w.astype(BF16) for w in (w_ffn_gate, w_ffn_up, w_ffn_down))
    gfin = final_norm_g.reshape(1, D).astype(F32)

    cosT, sinT, cos128, sin128 = _rope_tables(positions)
    for l in range(L):
        qaT, ka, vaT = _mla_in(x, gmix[l], pw["wa"][l], qg[l], kvg[l], pw["wqT"][l], pw["wkn"][l],
                               pw["wvT"][l], cosT, sinT, cos128, sin128, tm)
        oaT = _attention(qaT, ka, vaT, CHUNK)
        qbT, kb, vbT = _fox_in(x, gmix[l], pw["wqbT"][l], pw["wkb"][l], pw["wvbT"][l],
                               pw["wf"][l], pw["bf"][l], tm)
        obT = _attention(qbT, kb, vbT, 1)
        u = _glu(x, gmix[l], pw["wconv"][l], tm)
        uc = _conv(u, dw_kernel[l].astype(F32), dwb[l], lng[l], lnb[l], tm)
        x = _branch_out(x, gmix[l], pw["wgate"][l], bg[l], oaT, obT, uc, woa[l], wob[l], woc[l], wo[l])
        x = _ffn(x, gffn[l], wfg[l], wfu[l], wfd[l], gfin, tf, final=(l == L - 1))
    return x
```

```python
import functools
import math

import jax
import jax.numpy as jnp
from jax import lax
from jax.experimental import pallas as pl
from jax.experimental.pallas import tpu as pltpu

F32 = jnp.float32
BF16 = jnp.bfloat16

MLA_HEADS = 8
MLA_NOPE = 64
MLA_ROPE = 32
MLA_V = 64
Q_LORA = 384
KV_LORA = 256
ROPE_THETA = 10000.0
FOX_HEADS = 8
FOX_HEAD_DIM = 64
CONV_CHANNELS = 512
CONV_WIDTH = 31
CHUNK = 64
N_BRANCHES = 3
RMS_EPS = 1e-6
LN_EPS = 1e-5

HEADS = 8
HEAD_V = 64
VROWS = 80
KPAD = 128
FROWS = 16
T = 256
TQ = 512
G = 2
HALO = 32
LOG2E = math.log2(math.e)
NEG_BIG = -1e30
VMEM_LIMIT = 56 * 1024 * 1024


def _rms(x, g):
    ms = jnp.mean(x * x, axis=-1, keepdims=True)
    return x * lax.rsqrt(ms + RMS_EPS) * g


def _split3(v):
    hi = v.astype(BF16)
    r = v - hi.astype(F32)
    mid = r.astype(BF16)
    lo = (r - mid.astype(F32)).astype(BF16)
    return hi, mid, lo


def _dot(a, b):
    return jnp.dot(a, b, preferred_element_type=F32)


def _dot_nt(a, b):
    return lax.dot_general(a, b, (((1,), (1,)), ((), ())), preferred_element_type=F32)


def _dot_tn(a, b):
    return lax.dot_general(a, b, (((0,), (0,)), ((), ())), preferred_element_type=F32)


def _store_values(vT_ref, vT, tm):
    vT = vT.astype(BF16).reshape(HEADS, HEAD_V, tm)
    row = lax.broadcasted_iota(jnp.int32, (HEADS, VROWS - HEAD_V, T), 1)
    ones_row = jnp.where(row < 1, 1.0, 0.0).astype(BF16)
    for t in range(tm // T):
        vT_ref[t, :, 0:HEAD_V, :] = vT[:, :, t * T:(t + 1) * T]
        vT_ref[t, :, HEAD_V:, :] = ones_row


def _params(sem):
    return pltpu.CompilerParams(dimension_semantics=sem, vmem_limit_bytes=VMEM_LIMIT)


def _const_spec(shape):
    n = len(shape)
    return pl.BlockSpec(shape, lambda *_: (0,) * n)


def _rope_kernel(prow_ref, pcol_ref, invc_ref, invl_ref, sgn_ref, cosT_ref, sinT_ref, cos_ref, sin_ref):
    angT = prow_ref[...].astype(F32) * invc_ref[...]
    cosT_ref[...] = jnp.cos(angT)
    sinT_ref[...] = jnp.sin(angT)
    ang = pcol_ref[...].astype(F32) * invl_ref[...]
    sgn = sgn_ref[...]
    cos_ref[...] = jnp.cos(ang) * jnp.abs(sgn)
    sin_ref[...] = jnp.sin(ang) * sgn


def _rope_tables(positions):
    B, S = positions.shape
    half = MLA_ROPE // 2
    inv_freq = 1.0 / (ROPE_THETA ** (jnp.arange(0, MLA_ROPE, 2, dtype=F32) / MLA_ROPE))
    inv_col = inv_freq.reshape(half, 1)
    zeros = jnp.zeros((MLA_NOPE,), F32)
    tail = jnp.zeros((KPAD - MLA_NOPE - MLA_ROPE,), F32)
    inv_lane = jnp.concatenate([zeros, inv_freq, inv_freq, tail]).reshape(1, KPAD)
    ones = jnp.ones((half,), F32)
    sgn_lane = jnp.concatenate([zeros, -ones, ones, tail]).reshape(1, KPAD)
    TS = min(S, 1024)
    return pl.pallas_call(
        _rope_kernel,
        grid=(B, S // TS),
        in_specs=[
            pl.BlockSpec((None, 1, TS), lambda b, i: (b, 0, i)),
            pl.BlockSpec((None, TS, 1), lambda b, i: (b, i, 0)),
            _const_spec((half, 1)),
            _const_spec((1, KPAD)),
            _const_spec((1, KPAD)),
        ],
        out_specs=[
            pl.BlockSpec((None, half, TS), lambda b, i: (b, 0, i)),
            pl.BlockSpec((None, half, TS), lambda b, i: (b, 0, i)),
            pl.BlockSpec((None, TS, KPAD), lambda b, i: (b, i, 0)),
            pl.BlockSpec((None, TS, KPAD), lambda b, i: (b, i, 0)),
        ],
        out_shape=[
            jax.ShapeDtypeStruct((B, half, S), F32),
            jax.ShapeDtypeStruct((B, half, S), F32),
            jax.ShapeDtypeStruct((B, S, KPAD), F32),
            jax.ShapeDtypeStruct((B, S, KPAD), F32),
        ],
        compiler_params=_params(("parallel", "parallel")),
        name="rope_tables",
    )(positions.reshape(B, 1, S), positions.reshape(B, S, 1), inv_col, inv_lane, sgn_lane)


def _mla_in_kernel(x_ref, gmix_ref, wa_ref, qg_ref, kvg_ref, wqT_ref, wkn_ref, wvT_ref,
                   cosT_ref, sinT_ref, cos_ref, sin_ref, qT_ref, k_ref, vT_ref, *, tm):
    h = _rms(x_ref[...], gmix_ref[...]).astype(BF16)
    pa = _dot(h, wa_ref[...])
    cqn = _rms(pa[:, 0:Q_LORA], qg_ref[...]).astype(BF16)
    ckvn = _rms(pa[:, Q_LORA:Q_LORA + KV_LORA], kvg_ref[...]).astype(BF16)
    o = Q_LORA + KV_LORA
    kra = pa[:, o:o + KPAD]
    krb = pa[:, o + KPAD:o + 2 * KPAD]

    qT = _dot_nt(wqT_ref[...], cqn) * (LOG2E * (MLA_NOPE + MLA_ROPE) ** -0.5)
    qT = qT.reshape(HEADS, KPAD, tm)
    half = MLA_ROPE // 2
    x1 = qT[:, MLA_NOPE:MLA_NOPE + half, :]
    x2 = qT[:, MLA_NOPE + half:MLA_NOPE + 2 * half, :]
    c = cosT_ref[...][None]
    s = sinT_ref[...][None]
    qT = jnp.concatenate(
        [qT[:, :MLA_NOPE, :], x1 * c - x2 * s, x2 * c + x1 * s, qT[:, MLA_NOPE + 2 * half:, :]],
        axis=1).astype(BF16)
    for t in range(tm // TQ):
        qT_ref[t] = qT[:, :, t * TQ:(t + 1) * TQ]
    _store_values(vT_ref, _dot_nt(wvT_ref[...], ckvn), tm)

    krope = kra * cos_ref[...] + krb * sin_ref[...]
    kn = _dot(ckvn, wkn_ref[...])
    for hd in range(HEADS):
        k_ref[hd] = (kn[:, hd * KPAD:(hd + 1) * KPAD] + krope).astype(BF16)


def _mla_in(x, gmix, wa, qg, kvg, wqT, wkn, wvT, cosT, sinT, cos128, sin128, tm):
    B, S, D = x.shape
    nt = tm // T
    kern = functools.partial(_mla_in_kernel, tm=tm)
    half = MLA_ROPE // 2
    return pl.pallas_call(
        kern,
        grid=(B, S // tm),
        in_specs=[
            pl.BlockSpec((None, tm, D), lambda b, i: (b, i, 0)),
            _const_spec(gmix.shape), _const_spec(wa.shape), _const_spec(qg.shape),
            _const_spec(kvg.shape), _const_spec(wqT.shape), _const_spec(wkn.shape),
            _const_spec(wvT.shape),
            pl.BlockSpec((None, half, tm), lambda b, i: (b, 0, i)),
            pl.BlockSpec((None, half, tm), lambda b, i: (b, 0, i)),
            pl.BlockSpec((None, tm, KPAD), lambda b, i: (b, i, 0)),
            pl.BlockSpec((None, tm, KPAD), lambda b, i: (b, i, 0)),
        ],
        out_specs=[
            pl.BlockSpec((None, tm // TQ, HEADS, KPAD, TQ), lambda b, i: (b, i, 0, 0, 0)),
            pl.BlockSpec((None, HEADS, tm, KPAD), lambda b, i: (b, 0, i, 0)),
            pl.BlockSpec((None, nt, HEADS, VROWS, T), lambda b, i: (b, i, 0, 0, 0)),
        ],
        out_shape=[
            jax.ShapeDtypeStruct((B, S // TQ, HEADS, KPAD, TQ), BF16),
            jax.ShapeDtypeStruct((B, HEADS, S, KPAD), BF16),
            jax.ShapeDtypeStruct((B, S // T, HEADS, VROWS, T), BF16),
        ],
        compiler_params=_params(("parallel", "parallel")),
        name="mla_in",
    )(x, gmix, wa, qg, kvg, wqT, wkn, wvT, cosT, sinT, cos128, sin128)


def _fox_in_kernel(x_ref, gmix_ref, wqT_ref, wkp_ref, wvT_ref, wf_ref, bf_ref, tri_ref, e3_ref,
                   qT_ref, k_ref, vT_ref, carry_ref, *, tm):
    @pl.when(pl.program_id(1) == 0)
    def _():
        carry_ref[...] = jnp.zeros_like(carry_ref)

    h = _rms(x_ref[...], gmix_ref[...]).astype(BF16)
    qT = (_dot_nt(wqT_ref[...], h) * (LOG2E * FOX_HEAD_DIM ** -0.5)).astype(BF16)
    qT = qT.reshape(HEADS, FOX_HEAD_DIM, tm)
    row = lax.broadcasted_iota(jnp.int32, (HEADS, 16, TQ), 1)
    ones_rows = jnp.where(row < 3, 1.0, 0.0).astype(BF16)
    zero_rows = jnp.zeros((HEADS, KPAD - FOX_HEAD_DIM - 16, TQ), BF16)
    for t in range(tm // TQ):
        qT_ref[t, :, 0:FOX_HEAD_DIM, :] = qT[:, :, t * TQ:(t + 1) * TQ]
        qT_ref[t, :, FOX_HEAD_DIM:FOX_HEAD_DIM + 16, :] = ones_rows
        qT_ref[t, :, FOX_HEAD_DIM + 16:, :] = zero_rows
    _store_values(vT_ref, _dot_nt(wvT_ref[...], h), tm)

    f = _dot_nt(wf_ref[...], h) + bf_ref[...]
    frow = lax.broadcasted_iota(jnp.int32, f.shape, 0)
    logf = jnp.where(frow < FOX_HEADS, jax.nn.log_sigmoid(f), 0.0)
    c3 = _dot(jnp.concatenate(_split3(logf), axis=0), tri_ref[...])
    cum = c3[0:FROWS] + c3[FROWS:2 * FROWS] + c3[2 * FROWS:] + carry_ref[...]
    carry_ref[...] = jnp.broadcast_to(cum[:, tm - 1:tm], cum.shape)
    pieces = jnp.concatenate(
        _split3(cum * (-LOG2E)) + (jnp.zeros((KPAD - 3 * FROWS, tm), BF16),), axis=0)
    place = _dot_tn(pieces, e3_ref[...])
    kp = _dot(h, wkp_ref[...]) + place
    for hd in range(HEADS):
        k_ref[hd] = kp[:, hd * KPAD:(hd + 1) * KPAD].astype(BF16)


def _fox_in(x, gmix, wqT, wkp, wvT, wf, bf, tm):
    B, S, D = x.shape
    nt = tm // T
    r = lax.broadcasted_iota(jnp.int32, (tm, tm), 0)
    c = lax.broadcasted_iota(jnp.int32, (tm, tm), 1)
    tri = (r <= c).astype(BF16)
    bf = jnp.broadcast_to(bf, (FROWS, tm))
    rr = jnp.arange(KPAD)
    cc = jnp.arange(HEADS * KPAD)
    part, hd = rr // FROWS, rr % FROWS
    e3 = ((cc[None, :] == (hd * KPAD + FOX_HEAD_DIM + part)[:, None])
          & ((hd < HEADS) & (part < 3))[:, None]).astype(BF16)
    kern = functools.partial(_fox_in_kernel, tm=tm)
    return pl.pallas_call(
        kern,
        grid=(B, S // tm),
        in_specs=[
            pl.BlockSpec((None, tm, D), lambda b, i: (b, i, 0)),
            _const_spec(gmix.shape), _const_spec(wqT.shape), _const_spec(wkp.shape),
            _const_spec(wvT.shape), _const_spec(wf.shape), _const_spec(bf.shape),
            _const_spec(tri.shape), _const_spec(e3.shape),
        ],
        out_specs=[
            pl.BlockSpec((None, tm // TQ, HEADS, KPAD, TQ), lambda b, i: (b, i, 0, 0, 0)),
            pl.BlockSpec((None, HEADS, tm, KPAD), lambda b, i: (b, 0, i, 0)),
            pl.BlockSpec((None, nt, HEADS, VROWS, T), lambda b, i: (b, i, 0, 0, 0)),
        ],
        out_shape=[
            jax.ShapeDtypeStruct((B, S // TQ, HEADS, KPAD, TQ), BF16),
            jax.ShapeDtypeStruct((B, HEADS, S, KPAD), BF16),
            jax.ShapeDtypeStruct((B, S // T, HEADS, VROWS, T), BF16),
        ],
        scratch_shapes=[pltpu.VMEM((FROWS, tm), F32)],
        compiler_params=_params(("parallel", "arbitrary")),
        name="fox_in",
    )(x, gmix, wqT, wkp, wvT, wf, bf, tri, e3)


def _attn_kernel(qT_ref, k_ref, vT_ref, oT_ref, acc_ref, s_ref, *, nqb, chunk):
    assert TQ == 2 * T
    kpos = lax.broadcasted_iota(jnp.int32, (T, T), 0)
    qpos = lax.broadcasted_iota(jnp.int32, (T, T), 1)
    tile_mask = ((kpos // chunk) <= (qpos // chunk)) if chunk > 1 else (kpos <= qpos)

    def block_scores(qi, j, slot):
        for g in range(G):
            k = k_ref[g, pl.ds(pl.multiple_of(j * T, T), T), :]
            s_ref[slot, g] = _dot(k, qT_ref[qi, g])

    def q_block(qb, sa):
        sb = sa + 1
        scores = functools.partial(block_scores, qb)

        def process(j, slot, stats):
            out = []
            for g in range(G):
                m = stats[g]
                s = s_ref[slot, g]
                m_new = jnp.maximum(m, jnp.max(s, axis=0, keepdims=True))
                p = jnp.exp2(s - m_new).astype(BF16)
                pv = _dot(vT_ref[j, g], p)
                acc_ref[g] = jnp.exp2(m - m_new) * acc_ref[g] + pv
                out.append(m_new)
            return tuple(out)

        def last_scores(j):
            for g in range(G):
                k = k_ref[g, pl.ds(pl.multiple_of(j * T, T), T), :]
                s_ref[sb, g, :, T:] = _dot(k, qT_ref[qb, g, :, T:])

        def diagonal(j, stats):
            for g in range(G):
                m = stats[g]
                s0 = s_ref[sa, g]
                s0 = jnp.concatenate([jnp.where(tile_mask, s0[:, :T], NEG_BIG), s0[:, T:]], axis=1)
                s1 = jnp.where(tile_mask, s_ref[sb, g, :, T:], NEG_BIG)
                mx = jnp.max(s0, axis=0, keepdims=True)
                mx = jnp.concatenate(
                    [mx[:, :T], jnp.maximum(mx[:, T:], jnp.max(s1, axis=0, keepdims=True))], axis=1)
                m_new = jnp.maximum(m, mx)
                p0 = jnp.exp2(s0 - m_new).astype(BF16)
                p1 = jnp.exp2(s1 - m_new[:, T:]).astype(BF16)
                alpha = jnp.exp2(m - m_new)
                acc_ref[g] = alpha * acc_ref[g] + _dot(vT_ref[j, g], p0)
                acc_ref[g, :, T:] = acc_ref[g, :, T:] + _dot(vT_ref[j + 1, g], p1)

        def pair(jj, stats):
            j = 2 * jj
            scores(j + 1, sb)
            stats = process(j, sa, stats)
            scores(j + 2, sa)
            return process(j + 1, sb, stats)

        def quad(jj, stats):
            return pair(2 * jj + 1, pair(2 * jj, stats))

        acc_ref[...] = jnp.zeros_like(acc_ref)
        init = tuple(jnp.full((1, TQ), NEG_BIG, F32) for _ in range(G))

        def octet(jj, stats):
            return quad(2 * jj + 1, quad(2 * jj, stats))

        n8, n4 = qb // 4, qb // 2
        stats = lax.fori_loop(0, n8, octet, init)
        stats = lax.fori_loop(2 * n8, n4, quad, stats)
        stats = lax.fori_loop(2 * n4, qb, pair, stats)
        j = 2 * qb
        last_scores(j + 1)
        block_scores(jnp.minimum(qb + 1, nqb - 1), 0, 2 - sa)
        diagonal(j, stats)
        for g in range(G):
            a = acc_ref[g]
            oT_ref[qb, g * HEAD_V:(g + 1) * HEAD_V, :] = (a[:HEAD_V] / a[HEAD_V:HEAD_V + 1]).astype(BF16)

    def q_pair(i, _):
        q_block(2 * i, 0)
        q_block(2 * i + 1, 2)
        return 0

    assert nqb % 2 == 0
    block_scores(0, 0, 0)
    lax.fori_loop(0, nqb // 2, q_pair, 0)


def _attention(qT, k, vT, chunk):
    B, nqb = qT.shape[0], qT.shape[1]
    S = nqb * TQ
    kern = functools.partial(_attn_kernel, nqb=nqb, chunk=chunk)
    return pl.pallas_call(
        kern,
        grid=(B, HEADS // G),
        in_specs=[
            pl.BlockSpec((None, nqb, G, KPAD, TQ), lambda b, h: (b, 0, h, 0, 0)),
            pl.BlockSpec((None, G, S, KPAD), lambda b, h: (b, h, 0, 0)),
            pl.BlockSpec((None, S // T, G, VROWS, T), lambda b, h: (b, 0, h, 0, 0)),
        ],
        out_specs=pl.BlockSpec((None, nqb, G * HEAD_V, TQ), lambda b, h: (b, 0, h, 0)),
        out_shape=jax.ShapeDtypeStruct((B, nqb, HEADS * HEAD_V, TQ), BF16),
        scratch_shapes=[pltpu.VMEM((G, VROWS, TQ), F32), pltpu.VMEM((4, G, T, TQ), F32)],
        compiler_params=_params(("parallel", "parallel")),
        name="attn_chunk%d" % chunk,
    )(qT, k, vT)


def _conv_kernel(x_ref, gmix_ref, wc_ref, w_ref, b_ref, lng_ref, lnb_ref, o_ref, buf_ref, *, ts, rows):
    n = HALO + ts
    first = pl.program_id(1) == 0

    @pl.when(first)
    def _():
        buf_ref[0, 0:HALO, :] = jnp.zeros((HALO, CONV_CHANNELS), F32)

    @pl.when(jnp.logical_not(first))
    def _():
        buf_ref[0, 0:HALO, :] = buf_ref[0, ts:n, :]

    h = _rms(x_ref[...], gmix_ref[...]).astype(BF16)
    ci = _dot(h, wc_ref[...])
    buf_ref[0, HALO:n, :] = ci[:, :CONV_CHANNELS] * jax.nn.sigmoid(ci[:, CONV_CHANNELS:])
    for r in range(1, 8):
        buf_ref[r, 0:n - 8, :] = buf_ref[0, r:r + n - 8, :]
    off = HALO - (CONV_WIDTH - 1)
    w = w_ref[...]
    for c in range(ts // rows):
        acc = jnp.zeros((rows, CONV_CHANNELS), F32)
        for kk in range(CONV_WIDTH):
            a, r = divmod(off + kk, 8)
            base = c * rows + 8 * a
            acc = acc + buf_ref[r, base:base + rows, :] * w[kk:kk + 1, :]
        y = acc + b_ref[...]
        mu = jnp.mean(y, axis=-1, keepdims=True)
        d = y - mu
        var = jnp.mean(d * d, axis=-1, keepdims=True)
        z = d * lax.rsqrt(var + LN_EPS) * lng_ref[...] + lnb_ref[...]
        o_ref[c * rows:(c + 1) * rows, :] = (z * jax.nn.sigmoid(z)).astype(BF16)


def _conv(x, gmix, wc, w, b, lng, lnb, ts):
    B, S, D = x.shape
    C = CONV_CHANNELS
    rows = 32
    kern = functools.partial(_conv_kernel, ts=ts, rows=rows)
    return pl.pallas_call(
        kern,
        grid=(B, S // ts),
        in_specs=[
            pl.BlockSpec((None, ts, D), lambda b, i: (b, i, 0)),
            _const_spec(gmix.shape), _const_spec(wc.shape),
            _const_spec(w.shape), _const_spec(b.shape), _const_spec(lng.shape), _const_spec(lnb.shape),
        ],
        out_specs=pl.BlockSpec((None, ts, C), lambda b, i: (b, i, 0)),
        out_shape=jax.ShapeDtypeStruct((B, S, C), BF16),
        scratch_shapes=[pltpu.VMEM((8, HALO + ts, C), F32)],
        compiler_params=_params(("parallel", "arbitrary")),
        name="glu_conv",
    )(x, gmix, wc, w, b, lng, lnb)


def _branch_out_kernel(x_ref, gmix_ref, wg_ref, bg_ref, oaT_ref, obT_ref, uc_ref,
                       wa_ref, wb_ref, wc_ref, wo_ref, o_ref):
    x = x_ref[...]
    D = x.shape[-1]
    h = _rms(x, gmix_ref[...]).astype(BF16)
    g = jax.nn.sigmoid(_dot(h, wg_ref[...]) + bg_ref[...])
    y = g[:, 0:D] * _dot_tn(oaT_ref[...], wa_ref[...])
    y = y + g[:, D:2 * D] * _dot_tn(obT_ref[...], wb_ref[...])
    y = y + g[:, 2 * D:3 * D] * _dot(uc_ref[...], wc_ref[...])
    o_ref[...] = x + _dot(y.astype(BF16), wo_ref[...])


def _branch_out(x, gmix, wg, bg, oaT, obT, uc, wa, wb, wc, wo):
    B, S, D = x.shape
    C = HEADS * HEAD_V
    return pl.pallas_call(
        _branch_out_kernel,
        grid=(B, S // TQ),
        in_specs=[
            pl.BlockSpec((None, TQ, D), lambda b, i: (b, i, 0)),
            _const_spec(gmix.shape), _const_spec(wg.shape), _const_spec(bg.shape),
            pl.BlockSpec((None, None, C, TQ), lambda b, i: (b, i, 0, 0)),
            pl.BlockSpec((None, None, C, TQ), lambda b, i: (b, i, 0, 0)),
            pl.BlockSpec((None, TQ, CONV_CHANNELS), lambda b, i: (b, i, 0)),
            _const_spec(wa.shape), _const_spec(wb.shape), _const_spec(wc.shape), _const_spec(wo.shape),
        ],
        out_specs=pl.BlockSpec((None, TQ, D), lambda b, i: (b, i, 0)),
        out_shape=jax.ShapeDtypeStruct((B, S, D), F32),
        compiler_params=_params(("parallel", "parallel")),
        name="branch_out",
    )(x, gmix, wg, bg, oaT, obT, uc, wa, wb, wc, wo)


def _ffn_kernel(x_ref, g_ref, wg_ref, wu_ref, wd_ref, gout_ref, o_ref, *, final):
    x = x_ref[...]
    h = _rms(x, g_ref[...]).astype(BF16)
    a = _dot(h, wg_ref[...])
    u = _dot(h, wu_ref[...])
    ff = (a * jax.nn.sigmoid(a) * u).astype(BF16)
    y = x + _dot(ff, wd_ref[...])
    o_ref[...] = _rms(y, gout_ref[...]) if final else y


def _ffn(x, g, wg, wu, wd, gout, tm, final):
    B, S, D = x.shape
    return pl.pallas_call(
        functools.partial(_ffn_kernel, final=final),
        grid=(B, S // tm),
        in_specs=[pl.BlockSpec((None, tm, D), lambda b, i: (b, i, 0)),
                  _const_spec(g.shape), _const_spec(wg.shape), _const_spec(wu.shape), _const_spec(wd.shape),
                  _const_spec(gout.shape)],
        out_specs=pl.BlockSpec((None, tm, D), lambda b, i: (b, i, 0)),
        out_shape=jax.ShapeDtypeStruct((B, S, D), F32),
        compiler_params=_params(("parallel", "parallel")),
        name="ffn_final" if final else "ffn",
    )(x, g, wg, wu, wd, gout)


def _prep_weights(w_in, w_uq, w_ukv, b_forget):
    L, D, _ = w_in.shape
    w_in, w_uq, w_ukv = w_in.astype(BF16), w_uq.astype(BF16), w_ukv.astype(BF16)
    sizes = (Q_LORA, KV_LORA, MLA_ROPE, FOX_HEADS * FOX_HEAD_DIM, FOX_HEADS * FOX_HEAD_DIM,
             FOX_HEADS * FOX_HEAD_DIM, FOX_HEADS, 2 * CONV_CHANNELS, N_BRANCHES * D)
    offs = [0]
    for sz in sizes:
        offs.append(offs[-1] + sz)
    w_cq, w_ckv, w_kr, w_qb, w_kb, w_vb, w_f, w_conv, w_gate = [
        w_in[:, :, offs[i]:offs[i + 1]] for i in range(len(sizes))]
    half = MLA_ROPE // 2
    z = lambda n: jnp.zeros((L, D, n), w_in.dtype)
    kr_a = jnp.concatenate([z(MLA_NOPE), w_kr, z(KPAD - MLA_NOPE - MLA_ROPE)], axis=2)
    kr_b = jnp.concatenate([z(MLA_NOPE), w_kr[:, :, half:], w_kr[:, :, :half],
                            z(KPAD - MLA_NOPE - MLA_ROPE)], axis=2)
    wa = jnp.concatenate([w_cq, w_ckv, kr_a, kr_b], axis=2).astype(BF16)

    dq = MLA_NOPE + MLA_ROPE
    wq = w_uq.reshape(L, Q_LORA, MLA_HEADS, dq)
    wq = jnp.pad(wq, ((0, 0), (0, 0), (0, 0), (0, KPAD - dq)))
    wqT = wq.transpose(0, 2, 3, 1).reshape(L, MLA_HEADS * KPAD, Q_LORA).astype(BF16)
    wkv = w_ukv.reshape(L, KV_LORA, MLA_HEADS, MLA_NOPE + MLA_V)
    wkn = jnp.pad(wkv[..., :MLA_NOPE], ((0, 0), (0, 0), (0, 0), (0, KPAD - MLA_NOPE)))
    wkn = wkn.reshape(L, KV_LORA, MLA_HEADS * KPAD).astype(BF16)
    wvT = wkv[..., MLA_NOPE:].transpose(0, 2, 3, 1).reshape(L, MLA_HEADS * MLA_V, KV_LORA).astype(BF16)

    wqbT = w_qb.transpose(0, 2, 1).astype(BF16)
    wvbT = w_vb.transpose(0, 2, 1).astype(BF16)
    wkb = w_kb.reshape(L, D, FOX_HEADS, FOX_HEAD_DIM)
    wkb = jnp.pad(wkb, ((0, 0), (0, 0), (0, 0), (0, KPAD - FOX_HEAD_DIM)))
    wkb = wkb.reshape(L, D, FOX_HEADS * KPAD).astype(BF16)
    wf = jnp.pad(w_f.transpose(0, 2, 1), ((0, 0), (0, FROWS - FOX_HEADS), (0, 0))).astype(BF16)
    bf = jnp.pad(b_forget.astype(F32), ((0, 0), (0, FROWS - FOX_HEADS))).reshape(L, FROWS, 1)
    return dict(wa=wa, wqT=wqT, wkn=wkn, wvT=wvT, wqbT=wqbT, wkb=wkb, wvbT=wvbT, wf=wf, bf=bf,
                wconv=w_conv.astype(BF16), wgate=w_gate.astype(BF16))


def kernel(x, positions, norm_mix_g, w_in, b_gate, q_norm_g, w_uq, kv_norm_g, w_ukv, b_forget,
           dw_kernel, dw_bias, conv_ln_g, conv_ln_b, w_bo_a, w_bo_b, w_bo_c, w_out, norm_ffn_g,
           w_ffn_gate, w_ffn_up, w_ffn_down, final_norm_g):
    B, S, D = x.shape
    L = w_in.shape[0]
    assert S % TQ == 0 and D == 1024
    tm = TQ
    tf = T
    row = lambda a: a.reshape(L, 1, -1).astype(F32)
    pw = _prep_weights(w_in, w_uq, w_ukv, b_forget)
    gmix, bg, qg, kvg = row(norm_mix_g), row(b_gate), row(q_norm_g), row(kv_norm_g)
    dwb, lng, lnb, gffn = row(dw_bias), row(conv_ln_g), row(conv_ln_b), row(norm_ffn_g)
    woa, wob, woc, wo = (w.astype(BF16) for w in (w_bo_a, w_bo_b, w_bo_c, w_out))
    wfg, wfu, wfd = (w.astype(BF16) for w in (w_ffn_gate, w_ffn_up, w_ffn_down))
    gfin = final_norm_g.reshape(1, D).astype(F32)

    cosT, sinT, cos128, sin128 = _rope_tables(positions)
    for l in range(L):
        qaT, ka, vaT = _mla_in(x, gmix[l], pw["wa"][l], qg[l], kvg[l], pw["wqT"][l], pw["wkn"][l],
                               pw["wvT"][l], cosT, sinT, cos128, sin128, tm)
        oaT = _attention(qaT, ka, vaT, CHUNK)
        qbT, kb, vbT = _fox_in(x, gmix[l], pw["wqbT"][l], pw["wkb"][l], pw["wvbT"][l],
                               pw["wf"][l], pw["bf"][l], tm)
        obT = _attention(qbT, kb, vbT, 1)
        uc = _conv(x, gmix[l], pw["wconv"][l], dw_kernel[l].astype(F32), dwb[l], lng[l], lnb[l], tm)
        x = _branch_out(x, gmix[l], pw["wgate"][l], bg[l], oaT, obT, uc, woa[l], wob[l], woc[l], wo[l])
        x = _ffn(x, gffn[l], wfg[l], wfu[l], wfd[l], gfin, tf, final=(l == L - 1))
    return x
```

```python
import functools
import math

import jax
import jax.numpy as jnp
from jax import lax
from jax.experimental import pallas as pl
from jax.experimental.pallas import tpu as pltpu

F32 = jnp.float32
BF16 = jnp.bfloat16

MLA_HEADS = 8
MLA_NOPE = 64
MLA_ROPE = 32
MLA_V = 64
Q_LORA = 384
KV_LORA = 256
ROPE_THETA = 10000.0
FOX_HEADS = 8
FOX_HEAD_DIM = 64
CONV_CHANNELS = 512
CONV_WIDTH = 31
CHUNK = 64
N_BRANCHES = 3
RMS_EPS = 1e-6
LN_EPS = 1e-5

HEADS = 8
HEAD_V = 64
VROWS = 80
KPAD = 128
FROWS = 16
T = 256
TQ = 512
G = 2
HALO = 32
LOG2E = math.log2(math.e)
NEG_BIG = -1e30
VMEM_LIMIT = 56 * 1024 * 1024


def _rms(x, g):
    ms = jnp.mean(x * x, axis=-1, keepdims=True)
    return x * lax.rsqrt(ms + RMS_EPS) * g


def _split3(v):
    hi = v.astype(BF16)
    r = v - hi.astype(F32)
    mid = r.astype(BF16)
    lo = (r - mid.astype(F32)).astype(BF16)
    return hi, mid, lo


def _dot(a, b):
    return jnp.dot(a, b, preferred_element_type=F32)


def _dot_nt(a, b):
    return lax.dot_general(a, b, (((1,), (1,)), ((), ())), preferred_element_type=F32)


def _dot_tn(a, b):
    return lax.dot_general(a, b, (((0,), (0,)), ((), ())), preferred_element_type=F32)


def _store_values(vT_ref, vT, tm):
    vT = vT.astype(BF16).reshape(HEADS, HEAD_V, tm)
    row = lax.broadcasted_iota(jnp.int32, (HEADS, VROWS - HEAD_V, T), 1)
    ones_row = jnp.where(row < 1, 1.0, 0.0).astype(BF16)
    for t in range(tm // T):
        vT_ref[t, :, 0:HEAD_V, :] = vT[:, :, t * T:(t + 1) * T]
        vT_ref[t, :, HEAD_V:, :] = ones_row


def _params(sem):
    return pltpu.CompilerParams(dimension_semantics=sem, vmem_limit_bytes=VMEM_LIMIT)


def _const_spec(shape):
    n = len(shape)
    return pl.BlockSpec(shape, lambda *_: (0,) * n)


def _resident_spec(shape):
    n = len(shape)
    return pl.BlockSpec(shape, lambda *_: (0,) * n, pipeline_mode=pl.Buffered(1))


def _rope_kernel(prow_ref, pcol_ref, invc_ref, invl_ref, sgn_ref, cosT_ref, sinT_ref, cos_ref, sin_ref):
    angT = prow_ref[...].astype(F32) * invc_ref[...]
    cosT_ref[...] = jnp.cos(angT)
    sinT_ref[...] = jnp.sin(angT)
    ang = pcol_ref[...].astype(F32) * invl_ref[...]
    sgn = sgn_ref[...]
    cos_ref[...] = jnp.cos(ang) * jnp.abs(sgn)
    sin_ref[...] = jnp.sin(ang) * sgn


def _rope_tables(positions):
    B, S = positions.shape
    half = MLA_ROPE // 2
    inv_freq = 1.0 / (ROPE_THETA ** (jnp.arange(0, MLA_ROPE, 2, dtype=F32) / MLA_ROPE))
    inv_col = inv_freq.reshape(half, 1)
    zeros = jnp.zeros((MLA_NOPE,), F32)
    tail = jnp.zeros((KPAD - MLA_NOPE - MLA_ROPE,), F32)
    inv_lane = jnp.concatenate([zeros, inv_freq, inv_freq, tail]).reshape(1, KPAD)
    ones = jnp.ones((half,), F32)
    sgn_lane = jnp.concatenate([zeros, -ones, ones, tail]).reshape(1, KPAD)
    TS = min(S, 1024)
    return pl.pallas_call(
        _rope_kernel,
        grid=(B, S // TS),
        in_specs=[
            pl.BlockSpec((None, 1, TS), lambda b, i: (b, 0, i)),
            pl.BlockSpec((None, TS, 1), lambda b, i: (b, i, 0)),
            _const_spec((half, 1)),
            _const_spec((1, KPAD)),
            _const_spec((1, KPAD)),
        ],
        out_specs=[
            pl.BlockSpec((None, half, TS), lambda b, i: (b, 0, i)),
            pl.BlockSpec((None, half, TS), lambda b, i: (b, 0, i)),
            pl.BlockSpec((None, TS, KPAD), lambda b, i: (b, i, 0)),
            pl.BlockSpec((None, TS, KPAD), lambda b, i: (b, i, 0)),
        ],
        out_shape=[
            jax.ShapeDtypeStruct((B, half, S), F32),
            jax.ShapeDtypeStruct((B, half, S), F32),
            jax.ShapeDtypeStruct((B, S, KPAD), F32),
            jax.ShapeDtypeStruct((B, S, KPAD), F32),
        ],
        compiler_params=_params(("parallel", "parallel")),
        name="rope_tables",
    )(positions.reshape(B, 1, S), positions.reshape(B, S, 1), inv_col, inv_lane, sgn_lane)


def _mla_in_kernel(x_ref, gmix_ref, wa_ref, qg_ref, kvg_ref, wqT_ref, wkn_ref, wvT_ref,
                   cosT_ref, sinT_ref, cos_ref, sin_ref, qT_ref, k_ref, vT_ref, *, tm):
    h = _rms(x_ref[...], gmix_ref[...]).astype(BF16)
    pa = _dot(h, wa_ref[...])
    cqn = _rms(pa[:, 0:Q_LORA], qg_ref[...]).astype(BF16)
    ckvn = _rms(pa[:, Q_LORA:Q_LORA + KV_LORA], kvg_ref[...]).astype(BF16)
    o = Q_LORA + KV_LORA
    kra = pa[:, o:o + KPAD]
    krb = pa[:, o + KPAD:o + 2 * KPAD]

    qT = _dot_nt(wqT_ref[...], cqn) * (LOG2E * (MLA_NOPE + MLA_ROPE) ** -0.5)
    qT = qT.reshape(HEADS, KPAD, tm)
    half = MLA_ROPE // 2
    x1 = qT[:, MLA_NOPE:MLA_NOPE + half, :]
    x2 = qT[:, MLA_NOPE + half:MLA_NOPE + 2 * half, :]
    c = cosT_ref[...][None]
    s = sinT_ref[...][None]
    qT = jnp.concatenate(
        [qT[:, :MLA_NOPE, :], x1 * c - x2 * s, x2 * c + x1 * s, qT[:, MLA_NOPE + 2 * half:, :]],
        axis=1).astype(BF16)
    for t in range(tm // TQ):
        qT_ref[t] = qT[:, :, t * TQ:(t + 1) * TQ]
    _store_values(vT_ref, _dot_nt(wvT_ref[...], ckvn), tm)

    krope = kra * cos_ref[...] + krb * sin_ref[...]
    kn = _dot(ckvn, wkn_ref[...])
    for hd in range(HEADS):
        k_ref[hd] = (kn[:, hd * KPAD:(hd + 1) * KPAD] + krope).astype(BF16)


def _mla_in(x, gmix, wa, qg, kvg, wqT, wkn, wvT, cosT, sinT, cos128, sin128, tm):
    B, S, D = x.shape
    nt = tm // T
    kern = functools.partial(_mla_in_kernel, tm=tm)
    half = MLA_ROPE // 2
    return pl.pallas_call(
        kern,
        grid=(B, S // tm),
        in_specs=[
            pl.BlockSpec((None, tm, D), lambda b, i: (b, i, 0)),
            _const_spec(gmix.shape), _const_spec(wa.shape), _const_spec(qg.shape),
            _const_spec(kvg.shape), _const_spec(wqT.shape), _const_spec(wkn.shape),
            _const_spec(wvT.shape),
            pl.BlockSpec((None, half, tm), lambda b, i: (b, 0, i)),
            pl.BlockSpec((None, half, tm), lambda b, i: (b, 0, i)),
            pl.BlockSpec((None, tm, KPAD), lambda b, i: (b, i, 0)),
            pl.BlockSpec((None, tm, KPAD), lambda b, i: (b, i, 0)),
        ],
        out_specs=[
            pl.BlockSpec((None, tm // TQ, HEADS, KPAD, TQ), lambda b, i: (b, i, 0, 0, 0)),
            pl.BlockSpec((None, HEADS, tm, KPAD), lambda b, i: (b, 0, i, 0)),
            pl.BlockSpec((None, nt, HEADS, VROWS, T), lambda b, i: (b, i, 0, 0, 0)),
        ],
        out_shape=[
            jax.ShapeDtypeStruct((B, S // TQ, HEADS, KPAD, TQ), BF16),
            jax.ShapeDtypeStruct((B, HEADS, S, KPAD), BF16),
            jax.ShapeDtypeStruct((B, S // T, HEADS, VROWS, T), BF16),
        ],
        compiler_params=_params(("parallel", "parallel")),
        name="mla_in",
    )(x, gmix, wa, qg, kvg, wqT, wkn, wvT, cosT, sinT, cos128, sin128)


def _fox_in_kernel(x_ref, gmix_ref, wqT_ref, wkp_ref, wvT_ref, wf_ref, bf_ref, tri_ref, e3_ref,
                   qT_ref, k_ref, vT_ref, carry_ref, *, tm):
    @pl.when(pl.program_id(1) == 0)
    def _():
        carry_ref[...] = jnp.zeros_like(carry_ref)

    h = _rms(x_ref[...], gmix_ref[...]).astype(BF16)
    qT = (_dot_nt(wqT_ref[...], h) * (LOG2E * FOX_HEAD_DIM ** -0.5)).astype(BF16)
    qT = qT.reshape(HEADS, FOX_HEAD_DIM, tm)
    row = lax.broadcasted_iota(jnp.int32, (HEADS, 16, TQ), 1)
    ones_rows = jnp.where(row < 3, 1.0, 0.0).astype(BF16)
    zero_rows = jnp.zeros((HEADS, KPAD - FOX_HEAD_DIM - 16, TQ), BF16)
    for t in range(tm // TQ):
        qT_ref[t, :, 0:FOX_HEAD_DIM, :] = qT[:, :, t * TQ:(t + 1) * TQ]
        qT_ref[t, :, FOX_HEAD_DIM:FOX_HEAD_DIM + 16, :] = ones_rows
        qT_ref[t, :, FOX_HEAD_DIM + 16:, :] = zero_rows
    _store_values(vT_ref, _dot_nt(wvT_ref[...], h), tm)

    f = _dot_nt(wf_ref[...], h) + bf_ref[...]
    frow = lax.broadcasted_iota(jnp.int32, f.shape, 0)
    logf = jnp.where(frow < FOX_HEADS, jax.nn.log_sigmoid(f), 0.0)
    c3 = _dot(jnp.concatenate(_split3(logf), axis=0), tri_ref[...])
    cum = c3[0:FROWS] + c3[FROWS:2 * FROWS] + c3[2 * FROWS:] + carry_ref[...]
    carry_ref[...] = jnp.broadcast_to(cum[:, tm - 1:tm], cum.shape)
    pieces = jnp.concatenate(
        _split3(cum * (-LOG2E)) + (jnp.zeros((KPAD - 3 * FROWS, tm), BF16),), axis=0)
    place = _dot_tn(pieces, e3_ref[...])
    kp = _dot(h, wkp_ref[...]) + place
    for hd in range(HEADS):
        k_ref[hd] = kp[:, hd * KPAD:(hd + 1) * KPAD].astype(BF16)


def _fox_in(x, gmix, wqT, wkp, wvT, wf, bf, tm):
    B, S, D = x.shape
    nt = tm // T
    r = lax.broadcasted_iota(jnp.int32, (tm, tm), 0)
    c = lax.broadcasted_iota(jnp.int32, (tm, tm), 1)
    tri = (r <= c).astype(BF16)
    bf = jnp.broadcast_to(bf, (FROWS, tm))
    rr = jnp.arange(KPAD)
    cc = jnp.arange(HEADS * KPAD)
    part, hd = rr // FROWS, rr % FROWS
    e3 = ((cc[None, :] == (hd * KPAD + FOX_HEAD_DIM + part)[:, None])
          & ((hd < HEADS) & (part < 3))[:, None]).astype(BF16)
    kern = functools.partial(_fox_in_kernel, tm=tm)
    return pl.pallas_call(
        kern,
        grid=(B, S // tm),
        in_specs=[
            pl.BlockSpec((None, tm, D), lambda b, i: (b, i, 0)),
            _const_spec(gmix.shape), _const_spec(wqT.shape), _const_spec(wkp.shape),
            _const_spec(wvT.shape), _const_spec(wf.shape), _const_spec(bf.shape),
            _const_spec(tri.shape), _const_spec(e3.shape),
        ],
        out_specs=[
            pl.BlockSpec((None, tm // TQ, HEADS, KPAD, TQ), lambda b, i: (b, i, 0, 0, 0)),
            pl.BlockSpec((None, HEADS, tm, KPAD), lambda b, i: (b, 0, i, 0)),
            pl.BlockSpec((None, nt, HEADS, VROWS, T), lambda b, i: (b, i, 0, 0, 0)),
        ],
        out_shape=[
            jax.ShapeDtypeStruct((B, S // TQ, HEADS, KPAD, TQ), BF16),
            jax.ShapeDtypeStruct((B, HEADS, S, KPAD), BF16),
            jax.ShapeDtypeStruct((B, S // T, HEADS, VROWS, T), BF16),
        ],
        scratch_shapes=[pltpu.VMEM((FROWS, tm), F32)],
        compiler_params=_params(("parallel", "arbitrary")),
        name="fox_in",
    )(x, gmix, wqT, wkp, wvT, wf, bf, tri, e3)


def _attn_kernel(qT_ref, k_ref, vT_ref, oT_ref, acc_ref, s_ref, *, nqb, chunk):
    assert TQ == 2 * T
    kpos = lax.broadcasted_iota(jnp.int32, (T, T), 0)
    qpos = lax.broadcasted_iota(jnp.int32, (T, T), 1)
    tile_mask = ((kpos // chunk) <= (qpos // chunk)) if chunk > 1 else (kpos <= qpos)

    def block_scores(qi, j, slot):
        for g in range(G):
            k = k_ref[g, pl.ds(pl.multiple_of(j * T, T), T), :]
            s_ref[slot, g] = _dot(k, qT_ref[qi, g])

    def q_block(qb, sa):
        sb = sa + 1
        scores = functools.partial(block_scores, qb)

        def process(j, slot, stats):
            out = []
            for g in range(G):
                m = stats[g]
                s = s_ref[slot, g]
                m_new = jnp.maximum(m, jnp.max(s, axis=0, keepdims=True))
                p = jnp.exp2(s - m_new).astype(BF16)
                pv = _dot(vT_ref[j, g], p)
                acc_ref[g] = jnp.exp2(m - m_new) * acc_ref[g] + pv
                out.append(m_new)
            return tuple(out)

        def last_scores(j):
            for g in range(G):
                k = k_ref[g, pl.ds(pl.multiple_of(j * T, T), T), :]
                s_ref[sb, g, :, T:] = _dot(k, qT_ref[qb, g, :, T:])

        def diagonal(j, stats):
            for g in range(G):
                m = stats[g]
                s0 = s_ref[sa, g]
                s0 = jnp.concatenate([jnp.where(tile_mask, s0[:, :T], NEG_BIG), s0[:, T:]], axis=1)
                s1 = jnp.where(tile_mask, s_ref[sb, g, :, T:], NEG_BIG)
                mx = jnp.max(s0, axis=0, keepdims=True)
                mx = jnp.concatenate(
                    [mx[:, :T], jnp.maximum(mx[:, T:], jnp.max(s1, axis=0, keepdims=True))], axis=1)
                m_new = jnp.maximum(m, mx)
                p0 = jnp.exp2(s0 - m_new).astype(BF16)
                p1 = jnp.exp2(s1 - m_new[:, T:]).astype(BF16)
                alpha = jnp.exp2(m - m_new)
                acc_ref[g] = alpha * acc_ref[g] + _dot(vT_ref[j, g], p0)
                acc_ref[g, :, T:] = acc_ref[g, :, T:] + _dot(vT_ref[j + 1, g], p1)

        def pair(jj, stats):
            j = 2 * jj
            scores(j + 1, sb)
            stats = process(j, sa, stats)
            scores(j + 2, sa)
            return process(j + 1, sb, stats)

        def quad(jj, stats):
            return pair(2 * jj + 1, pair(2 * jj, stats))

        acc_ref[...] = jnp.zeros_like(acc_ref)
        init = tuple(jnp.full((1, TQ), NEG_BIG, F32) for _ in range(G))

        def octet(jj, stats):
            return quad(2 * jj + 1, quad(2 * jj, stats))

        n8, n4 = qb // 4, qb // 2
        stats = lax.fori_loop(0, n8, octet, init)
        stats = lax.fori_loop(2 * n8, n4, quad, stats)
        stats = lax.fori_loop(2 * n4, qb, pair, stats)
        j = 2 * qb
        last_scores(j + 1)
        block_scores(jnp.minimum(qb + 1, nqb - 1), 0, 2 - sa)
        diagonal(j, stats)
        for g in range(G):
            a = acc_ref[g]
            oT_ref[qb, g * HEAD_V:(g + 1) * HEAD_V, :] = (a[:HEAD_V] / a[HEAD_V:HEAD_V + 1]).astype(BF16)

    def q_pair(i, _):
        q_block(2 * i, 0)
        q_block(2 * i + 1, 2)
        return 0

    assert nqb % 2 == 0
    block_scores(0, 0, 0)
    lax.fori_loop(0, nqb // 2, q_pair, 0)


def _attention(qT, k, vT, chunk):
    B, nqb = qT.shape[0], qT.shape[1]
    S = nqb * TQ
    kern = functools.partial(_attn_kernel, nqb=nqb, chunk=chunk)
    return pl.pallas_call(
        kern,
        grid=(B, HEADS // G),
        in_specs=[
            pl.BlockSpec((None, nqb, G, KPAD, TQ), lambda b, h: (b, 0, h, 0, 0)),
            pl.BlockSpec((None, G, S, KPAD), lambda b, h: (b, h, 0, 0)),
            pl.BlockSpec((None, S // T, G, VROWS, T), lambda b, h: (b, 0, h, 0, 0)),
        ],
        out_specs=pl.BlockSpec((None, nqb, G * HEAD_V, TQ), lambda b, h: (b, 0, h, 0)),
        out_shape=jax.ShapeDtypeStruct((B, nqb, HEADS * HEAD_V, TQ), BF16),
        scratch_shapes=[pltpu.VMEM((G, VROWS, TQ), F32), pltpu.VMEM((4, G, T, TQ), F32)],
        compiler_params=_params(("parallel", "parallel")),
        name="attn_chunk%d" % chunk,
    )(qT, k, vT)


def _conv_kernel(x_ref, gmix_ref, wc_ref, w_ref, b_ref, lng_ref, lnb_ref, o_ref, buf_ref, *, ts, rows):
    n = HALO + ts
    first = pl.program_id(1) == 0

    @pl.when(first)
    def _():
        buf_ref[0, 0:HALO, :] = jnp.zeros((HALO, CONV_CHANNELS), F32)

    @pl.when(jnp.logical_not(first))
    def _():
        buf_ref[0, 0:HALO, :] = buf_ref[0, ts:n, :]

    h = _rms(x_ref[...], gmix_ref[...]).astype(BF16)
    ci = _dot(h, wc_ref[...])
    buf_ref[0, HALO:n, :] = ci[:, :CONV_CHANNELS] * jax.nn.sigmoid(ci[:, CONV_CHANNELS:])
    for r in range(1, 8):
        buf_ref[r, 0:n - 8, :] = buf_ref[0, r:r + n - 8, :]
    off = HALO - (CONV_WIDTH - 1)
    w = w_ref[...]
    for c in range(ts // rows):
        acc = jnp.zeros((rows, CONV_CHANNELS), F32)
        for kk in range(CONV_WIDTH):
            a, r = divmod(off + kk, 8)
            base = c * rows + 8 * a
            acc = acc + buf_ref[r, base:base + rows, :] * w[kk:kk + 1, :]
        y = acc + b_ref[...]
        mu = jnp.mean(y, axis=-1, keepdims=True)
        d = y - mu
        var = jnp.mean(d * d, axis=-1, keepdims=True)
        z = d * lax.rsqrt(var + LN_EPS) * lng_ref[...] + lnb_ref[...]
        o_ref[c * rows:(c + 1) * rows, :] = (z * jax.nn.sigmoid(z)).astype(BF16)


def _conv(x, gmix, wc, w, b, lng, lnb, ts):
    B, S, D = x.shape
    C = CONV_CHANNELS
    rows = 32
    kern = functools.partial(_conv_kernel, ts=ts, rows=rows)
    return pl.pallas_call(
        kern,
        grid=(B, S // ts),
        in_specs=[
            pl.BlockSpec((None, ts, D), lambda b, i: (b, i, 0)),
            _const_spec(gmix.shape), _const_spec(wc.shape),
            _const_spec(w.shape), _const_spec(b.shape), _const_spec(lng.shape), _const_spec(lnb.shape),
        ],
        out_specs=pl.BlockSpec((None, ts, C), lambda b, i: (b, i, 0)),
        out_shape=jax.ShapeDtypeStruct((B, S, C), BF16),
        scratch_shapes=[pltpu.VMEM((8, HALO + ts, C), F32)],
        compiler_params=_params(("parallel", "arbitrary")),
        name="glu_conv",
    )(x, gmix, wc, w, b, lng, lnb)


def _branch_out_kernel(x_ref, gmix_ref, wg_ref, bg_ref, oaT_ref, obT_ref, uc_ref,
                       wa_ref, wb_ref, wc_ref, wo_ref, o_ref):
    x = x_ref[...]
    D = x.shape[-1]
    h = _rms(x, gmix_ref[...]).astype(BF16)
    g = jax.nn.sigmoid(_dot(h, wg_ref[...]) + bg_ref[...])
    y = g[:, 0:D] * _dot_tn(oaT_ref[...], wa_ref[...])
    y = y + g[:, D:2 * D] * _dot_tn(obT_ref[...], wb_ref[...])
    y = y + g[:, 2 * D:3 * D] * _dot(uc_ref[...], wc_ref[...])
    o_ref[...] = x + _dot(y.astype(BF16), wo_ref[...])


def _branch_out(x, gmix, wg, bg, oaT, obT, uc, wa, wb, wc, wo):
    B, S, D = x.shape
    C = HEADS * HEAD_V
    return pl.pallas_call(
        _branch_out_kernel,
        grid=(B, S // TQ),
        in_specs=[
            pl.BlockSpec((None, TQ, D), lambda b, i: (b, i, 0)),
            _const_spec(gmix.shape), _const_spec(wg.shape), _const_spec(bg.shape),
            pl.BlockSpec((None, None, C, TQ), lambda b, i: (b, i, 0, 0)),
            pl.BlockSpec((None, None, C, TQ), lambda b, i: (b, i, 0, 0)),
            pl.BlockSpec((None, TQ, CONV_CHANNELS), lambda b, i: (b, i, 0)),
            _const_spec(wa.shape), _const_spec(wb.shape), _const_spec(wc.shape), _const_spec(wo.shape),
        ],
        out_specs=pl.BlockSpec((None, TQ, D), lambda b, i: (b, i, 0)),
        out_shape=jax.ShapeDtypeStruct((B, S, D), F32),
        compiler_params=_params(("parallel", "parallel")),
        name="branch_out",
    )(x, gmix, wg, bg, oaT, obT, uc, wa, wb, wc, wo)


def _ffn_kernel(x_ref, g_ref, wg_ref, wu_ref, wd_ref, gout_ref, o_ref, *, final):
    x = x_ref[...]
    h = _rms(x, g_ref[...]).astype(BF16)
    a = _dot(h, wg_ref[...])
    u = _dot(h, wu_ref[...])
    ff = (a * jax.nn.sigmoid(a) * u).astype(BF16)
    y = x + _dot(ff, wd_ref[...])
    o_ref[...] = _rms(y, gout_ref[...]) if final else y


def _ffn(x, g, wg, wu, wd, gout, tm, final):
    B, S, D = x.shape
    return pl.pallas_call(
        functools.partial(_ffn_kernel, final=final),
        grid=(B, S // tm),
        in_specs=[pl.BlockSpec((None, tm, D), lambda b, i: (b, i, 0)),
                  _const_spec(g.shape), _resident_spec(wg.shape), _resident_spec(wu.shape),
                  _resident_spec(wd.shape), _const_spec(gout.shape)],
        out_specs=pl.BlockSpec((None, tm, D), lambda b, i: (b, i, 0)),
        out_shape=jax.ShapeDtypeStruct((B, S, D), F32),
        compiler_params=_params(("parallel", "parallel")),
        name="ffn_final" if final else "ffn",
    )(x, g, wg, wu, wd, gout)


def _prep_weights(w_in, w_uq, w_ukv, b_forget):
    L, D, _ = w_in.shape
    w_in, w_uq, w_ukv = w_in.astype(BF16), w_uq.astype(BF16), w_ukv.astype(BF16)
    sizes = (Q_LORA, KV_LORA, MLA_ROPE, FOX_HEADS * FOX_HEAD_DIM, FOX_HEADS * FOX_HEAD_DIM,
             FOX_HEADS * FOX_HEAD_DIM, FOX_HEADS, 2 * CONV_CHANNELS, N_BRANCHES * D)
    offs = [0]
    for sz in sizes:
        offs.append(offs[-1] + sz)
    w_cq, w_ckv, w_kr, w_qb, w_kb, w_vb, w_f, w_conv, w_gate = [
        w_in[:, :, offs[i]:offs[i + 1]] for i in range(len(sizes))]
    half = MLA_ROPE // 2
    z = lambda n: jnp.zeros((L, D, n), w_in.dtype)
    kr_a = jnp.concatenate([z(MLA_NOPE), w_kr, z(KPAD - MLA_NOPE - MLA_ROPE)], axis=2)
    kr_b = jnp.concatenate([z(MLA_NOPE), w_kr[:, :, half:], w_kr[:, :, :half],
                            z(KPAD - MLA_NOPE - MLA_ROPE)], axis=2)
    wa = jnp.concatenate([w_cq, w_ckv, kr_a, kr_b], axis=2).astype(BF16)

    dq = MLA_NOPE + MLA_ROPE
    wq = w_uq.reshape(L, Q_LORA, MLA_HEADS, dq)
    wq = jnp.pad(wq, ((0, 0), (0, 0), (0, 0), (0, KPAD - dq)))
    wqT = wq.transpose(0, 2, 3, 1).reshape(L, MLA_HEADS * KPAD, Q_LORA).astype(BF16)
    wkv = w_ukv.reshape(L, KV_LORA, MLA_HEADS, MLA_NOPE + MLA_V)
    wkn = jnp.pad(wkv[..., :MLA_NOPE], ((0, 0), (0, 0), (0, 0), (0, KPAD - MLA_NOPE)))
    wkn = wkn.reshape(L, KV_LORA, MLA_HEADS * KPAD).astype(BF16)
    wvT = wkv[..., MLA_NOPE:].transpose(0, 2, 3, 1).reshape(L, MLA_HEADS * MLA_V, KV_LORA).astype(BF16)

    wqbT = w_qb.transpose(0, 2, 1).astype(BF16)
    wvbT = w_vb.transpose(0, 2, 1).astype(BF16)
    wkb = w_kb.reshape(L, D, FOX_HEADS, FOX_HEAD_DIM)
    wkb = jnp.pad(wkb, ((0, 0), (0, 0), (0, 0), (0, KPAD - FOX_HEAD_DIM)))
    wkb = wkb.reshape(L, D, FOX_HEADS * KPAD).astype(BF16)
    wf = jnp.pad(w_f.transpose(0, 2, 1), ((0, 0), (0, FROWS - FOX_HEADS), (0, 0))).astype(BF16)
    bf = jnp.pad(b_forget.astype(F32), ((0, 0), (0, FROWS - FOX_HEADS))).reshape(L, FROWS, 1)
    return dict(wa=wa, wqT=wqT, wkn=wkn, wvT=wvT, wqbT=wqbT, wkb=wkb, wvbT=wvbT, wf=wf, bf=bf,
                wconv=w_conv.astype(BF16), wgate=w_gate.astype(BF16))


def kernel(x, positions, norm_mix_g, w_in, b_gate, q_norm_g, w_uq, kv_norm_g, w_ukv, b_forget,
           dw_kernel, dw_bias, conv_ln_g, conv_ln_b, w_bo_a, w_bo_b, w_bo_c, w_out, norm_ffn_g,
           w_ffn_gate, w_ffn_up, w_ffn_down, final_norm_g):
    B, S, D = x.shape
    L = w_in.shape[0]
    assert S % TQ == 0 and D == 1024
    tm = TQ
    tf = TQ
    row = lambda a: a.reshape(L, 1, -1).astype(F32)
    pw = _prep_weights(w_in, w_uq, w_ukv, b_forget)
    gmix, bg, qg, kvg = row(norm_mix_g), row(b_gate), row(q_norm_g), row(kv_norm_g)
    dwb, lng, lnb, gffn = row(dw_bias), row(conv_ln_g), row(conv_ln_b), row(norm_ffn_g)
    woa, wob, woc, wo = (w.astype(BF16) for w in (w_bo_a, w_bo_b, w_bo_c, w_out))
    wfg, wfu, wfd = (w.astype(BF16) for w in (w_ffn_gate, w_ffn_up, w_ffn_down))
    gfin = final_norm_g.reshape(1, D).astype(F32)

    cosT, sinT, cos128, sin128 = _rope_tables(positions)
    for l in range(L):
        qaT, ka, vaT = _mla_in(x, gmix[l], pw["wa"][l], qg[l], kvg[l], pw["wqT"][l], pw["wkn"][l],
                               pw["wvT"][l], cosT, sinT, cos128, sin128, tm)
        oaT = _attention(qaT, ka, vaT, CHUNK)
        qbT, kb, vbT = _fox_in(x, gmix[l], pw["wqbT"][l], pw["wkb"][l], pw["wvbT"][l],
                               pw["wf"][l], pw["bf"][l], tm)
        obT = _attention(qbT, kb, vbT, 1)
        uc = _conv(x, gmix[l], pw["wconv"][l], dw_kernel[l].astype(F32), dwb[l], lng[l], lnb[l], tm)
        x = _branch_out(x, gmix[l], pw["wgate"][l], bg[l], oaT, obT, uc, woa[l], wob[l], woc[l], wo[l])
        x = _ffn(x, gffn[l], wfg[l], wfu[l], wfd[l], gfin, tf, final=(l == L - 1))
    return x
```
